```python
import jax, jax.numpy as jnp
from jax import lax
import numpy as np

D_MODEL = 4096
BATCH = 4
SEQ = 4096
DEPTH = 2
DEC_BATCH = 8
DEC_SEQ = 32
PAST_LEN = 4096

CHUNK = 64
EPS = 1e-6
RET_HEADS = 8
RET_HD = 256
RET_W = RET_HEADS * RET_HD
ROPE_BASE = 10000.0
CMLP_W = D_MODEL // 2
CMLP_GROUPS = 4
CMLP_GD = CMLP_W // CMLP_GROUPS
CMLP_CHUNK = 128
FOX_HEADS = 32
FOX_HD = D_MODEL // FOX_HEADS
FOX_W = FOX_HEADS * FOX_HD
Q_BLOCK = 128
MEM_LEN = 256
MEM_HEADS = 4
MEM_HD = 128
MEM_W = MEM_HEADS * MEM_HD
N_GROUPS = 4
EXP_PER_GROUP = 8
TOP_K = 2
D_EXPERT = 512
N_EVEN = (DEPTH + 1) // 2
N_ODD = DEPTH // 2
EVEN_IN = 4 * RET_W + 2 * CMLP_W
ODD_IN = 3 * FOX_W + FOX_HEADS

kernel_name = "hybrid_retention_gmlp_fox_hmoe_stream_step"

F32 = jnp.float32


def rms_norm(x, g):
    xf = x.astype(F32)
    y = xf * lax.rsqrt(jnp.mean(xf * xf, axis=-1, keepdims=True) + EPS)
    return (y * g.astype(F32)).astype(x.dtype)


def layer_norm(x, g, b):
    xf = x.astype(F32)
    xc = xf - jnp.mean(xf, axis=-1, keepdims=True)
    y = xc * lax.rsqrt(jnp.mean(xc * xc, axis=-1, keepdims=True) + EPS) * g.astype(F32) + b.astype(F32)
    return y.astype(x.dtype)


def group_norm(x, g):
    xc = x - jnp.mean(x, axis=-1, keepdims=True)
    return xc * lax.rsqrt(jnp.mean(xc * xc, axis=-1, keepdims=True) + EPS) * g.astype(F32)


def rope(x, pos):
    half = x.shape[-1] // 2
    inv = ROPE_BASE ** (-jnp.arange(half, dtype=F32) / half)
    ang = pos.astype(F32)[:, None] * inv[None, :]
    cos = jnp.cos(ang)[:, None, :]
    sin = jnp.sin(ang)[:, None, :]
    xf = x.astype(F32)
    x1, x2 = xf[..., :half], xf[..., half:]
    return jnp.concatenate([x1 * cos - x2 * sin, x1 * sin + x2 * cos], axis=-1)


def ret_log_gamma():
    return jnp.log1p(-jnp.exp2(-5.0 - jnp.arange(RET_HEADS, dtype=F32)))


def retention_block(q, k, v, S):
    L = q.shape[1]
    lg = ret_log_gamma()
    n = jnp.arange(L, dtype=F32)
    diff = n[:, None] - n[None, :]
    d_in = jnp.exp(jnp.where((diff >= 0)[None], diff[None] * lg[:, None, None], -jnp.inf))
    s = jnp.einsum('bnhd,bmhd->bhnm', q, k) * d_in
    inner = jnp.einsum('bhnm,bmhe->bnhe', s, v)
    q_dec = jnp.exp((n[:, None] + 1.0) * lg[None, :])[None, :, :, None]
    cross = jnp.einsum('bnhd,bhde->bnhe', q, S) * q_dec
    k_dec = jnp.exp((L - 1.0 - n)[:, None] * lg[None, :])[None, :, :, None]
    S_new = jnp.exp(L * lg)[None, :, None, None] * S + jnp.einsum('bmhd,bmhe->bhde', k * k_dec, v)
    return inner + cross, S_new


def retention(q, k, v, S0):
    B, T, H, hd = q.shape
    L = min(T, CHUNK)
    nc = T // L

    def to_chunks(a):
        return a.reshape(B, nc, L, H, hd).swapaxes(0, 1)

    def step(S, qkv):
        qc, kc, vc = qkv
        o, S_new = retention_block(qc, kc, vc, S)
        return S_new, o

    S, o = lax.scan(step, S0, (to_chunks(q), to_chunks(k), to_chunks(v)))
    return o.swapaxes(0, 1).reshape(B, T, H, hd), S


def even_mixer(h, pos, S0, w_in, gn_g, ln_g, ln_b, ws, bs, w_out):
    B, T, _ = h.shape
    proj = h @ w_in
    q, k, v, gate_a, u, vb = jnp.split(
        proj, [RET_W, 2 * RET_W, 3 * RET_W, 4 * RET_W, 4 * RET_W + CMLP_W], axis=-1)
    shp = (B, T, RET_HEADS, RET_HD)
    q = rope(q.reshape(shp), pos)
    k = rope(k.reshape(shp), pos) * (RET_HD ** -0.5)
    o, S = retention(q, k, v.reshape(shp).astype(F32), S0.astype(F32))
    o = group_norm(o, gn_g.reshape(RET_HEADS, RET_HD)).reshape(B, T, RET_W)
    y_a = (jax.nn.silu(gate_a.astype(F32)) * o).astype(h.dtype)
    zu = jax.nn.gelu(u)
    zv = layer_norm(jax.nn.gelu(vb), ln_g, ln_b)
    L = min(T, CMLP_CHUNK)
    nc = T // L
    w_mask = jnp.tril(ws[:, :L, :L])
    mixed = jnp.einsum('gts,bnsgc->bntgc', w_mask, zv.reshape(B, nc, L, CMLP_GROUPS, CMLP_GD))
    mixed = mixed + bs[:, :L].T[None, None, :, :, None]
    y_b = zu * mixed.reshape(B, T, CMLP_W)
    y = jnp.concatenate([y_a, y_b], axis=-1) @ w_out
    return y, S.astype(S0.dtype), zv


def fox_project(h, w_in, b_f, qn_g, kn_g):
    B, T, _ = h.shape
    proj = h @ w_in
    q, k, v, fl = jnp.split(proj, [FOX_W, 2 * FOX_W, 3 * FOX_W], axis=-1)
    shp = (B, T, FOX_HEADS, FOX_HD)
    q = rms_norm(q.reshape(shp), qn_g)
    k = rms_norm(k.reshape(shp), kn_g)
    logf = jax.nn.log_sigmoid((fl + b_f).astype(F32))
    return q, k, v.reshape(shp), logf


def fox_prompt(q, k, v, logf):
    B, T, H, hd = q.shape
    nb = T // Q_BLOCK
    c = jnp.cumsum(logf, axis=1).swapaxes(1, 2)
    qb = q.reshape(B, nb, Q_BLOCK, H, hd).swapaxes(0, 1)
    cb = c.reshape(B, H, nb, Q_BLOCK).transpose(2, 0, 1, 3)
    tb = jnp.arange(T).reshape(nb, Q_BLOCK)
    kpos = jnp.arange(T)
    scale = hd ** -0.5

    def block(args):
        qi, ci, ti = args
        s = jnp.einsum('bqhd,bkhd->bhqk', qi, k, preferred_element_type=F32) * scale
        s = s + ci[..., :, None] - c[..., None, :]
        s = jnp.where(ti[:, None] >= kpos[None, :], s, -jnp.inf)
        p = jax.nn.softmax(s, axis=-1).astype(v.dtype)
        return jnp.einsum('bhqk,bkhd->bqhd', p, v)

    o = lax.map(block, (qb, cb, tb))
    return o.swapaxes(0, 1).reshape(B, T, H * hd)


def fox_sample(q, k, v, logf, k_past, v_past, logf_past):
    B, T, H, hd = q.shape
    P = k_past.shape[1]
    c = jnp.cumsum(jnp.concatenate([logf_past.astype(F32), logf], axis=1), axis=1).swapaxes(1, 2)
    c_new = c[..., P:]
    scale = hd ** -0.5
    s_past = jnp.einsum('bqhd,bkhd->bhqk', q, k_past, preferred_element_type=F32) * scale
    s_past = s_past + c_new[..., :, None] - c[..., None, :P]
    s_new = jnp.einsum('bqhd,bkhd->bhqk', q, k, preferred_element_type=F32) * scale
    s_new = s_new + c_new[..., :, None] - c_new[..., None, :]
    s_new = jnp.where(jnp.tril(jnp.ones((T, T), dtype=bool)), s_new, -jnp.inf)
    p = jax.nn.softmax(jnp.concatenate([s_past, s_new], axis=-1), axis=-1)
    o = (jnp.einsum('bhqk,bkhd->bqhd', p[..., :P].astype(v_past.dtype), v_past)
         + jnp.einsum('bhqk,bkhd->bqhd', p[..., P:].astype(v.dtype), v))
    return o.reshape(B, T, H * hd)


def memory_kv(mem, src_g, w_k, w_v, kn_g):
    B, M, _ = mem.shape
    m = rms_norm(mem, src_g)
    shp = (B, M, MEM_HEADS, MEM_HD)
    k = rms_norm((m @ w_k).reshape(shp), kn_g)
    v = (m @ w_v).reshape(shp)
    return k, v


def memory_attn(h, mk, mv, w_q, qn_g, w_o):
    B, T, _ = h.shape
    q = rms_norm((h @ w_q).reshape(B, T, MEM_HEADS, MEM_HD), qn_g)
    s = jnp.einsum('bqhd,bkhd->bhqk', q, mk, preferred_element_type=F32) * (MEM_HD ** -0.5)
    p = jax.nn.softmax(s, axis=-1).astype(mv.dtype)
    o = jnp.einsum('bhqk,bkhd->bqhd', p, mv).reshape(B, T, MEM_W)
    return o @ w_o


def hier_moe(h, w_rg, b_rg, w_re, b_re, w_gate, w_up, w_down):
    B, T, D = h.shape
    N = B * T
    x = h.reshape(N, D)
    lg = jnp.dot(x, w_rg, preferred_element_type=F32) + b_rg.astype(F32)
    pg = jax.nn.softmax(lg, axis=-1)
    gsel = jnp.argmax(lg, axis=-1)
    p_grp = jnp.take_along_axis(pg, gsel[:, None], axis=-1)
    le = (jnp.dot(x, w_re, preferred_element_type=F32) + b_re.astype(F32)).reshape(N, N_GROUPS, EXP_PER_GROUP)
    le_sel = jnp.take_along_axis(le, gsel[:, None, None], axis=1)[:, 0]
    top_v, top_i = lax.top_k(le_sel, TOP_K)
    gate = jax.nn.softmax(top_v, axis=-1) * p_grp
    w_exp = jnp.einsum('nk,nke->ne', gate, jax.nn.one_hot(top_i, EXP_PER_GROUP, dtype=F32))
    combine = (jax.nn.one_hot(gsel, N_GROUPS, dtype=F32)[:, :, None] * w_exp[:, None, :]).astype(h.dtype)
    y = jnp.zeros_like(x)
    for g in range(N_GROUPS):
        a = jax.nn.silu(jnp.einsum('nd,edf->nef', x, w_gate[g])) * jnp.einsum('nd,edf->nef', x, w_up[g])
        y = y + jnp.einsum('nef,efd->nd', a * combine[:, g, :, None], w_down[g])
    return y.reshape(B, T, D)


def setup_inputs(seed: int = 0) -> dict:
    key = jax.random.key(seed)
    ks = iter(list(jax.random.split(key, 48)))

    def nrm(shape, scale=1.0):
        return jax.random.normal(next(ks), shape, F32) * scale

    def gain(shape):
        return 1.0 + nrm(shape, 0.02)

    G, E, F = N_GROUPS, EXP_PER_GROUP, D_EXPERT
    return {
        'x_prompt': nrm((BATCH, SEQ, D_MODEL)),
        'x_sample': nrm((DEC_BATCH, DEC_SEQ, D_MODEL)),
        'mem_prompt': nrm((BATCH, MEM_LEN, D_MODEL)),
        'state_ret': nrm((N_EVEN, DEC_BATCH, RET_HEADS, RET_HD, RET_HD), 0.05),
        'cache_fox_k': nrm((N_ODD, DEC_BATCH, PAST_LEN, FOX_HEADS, FOX_HD)),
        'cache_fox_v': nrm((N_ODD, DEC_BATCH, PAST_LEN, FOX_HEADS, FOX_HD)),
        'cache_fox_logf': jax.nn.log_sigmoid(3.0 + nrm((N_ODD, DEC_BATCH, PAST_LEN, FOX_HEADS))),
        'cache_mem_k': nrm((DEPTH, DEC_BATCH, MEM_LEN, MEM_HEADS, MEM_HD)),
        'cache_mem_v': nrm((DEPTH, DEC_BATCH, MEM_LEN, MEM_HEADS, MEM_HD)),
        'norm_mix': gain((DEPTH, D_MODEL)),
        'norm_mem': gain((DEPTH, D_MODEL)),
        'norm_moe': gain((DEPTH, D_MODEL)),
        'even_w_in': nrm((N_EVEN, D_MODEL, EVEN_IN), D_MODEL ** -0.5),
        'ret_gn_g': gain((N_EVEN, RET_W)),
        'cmlp_ln_g': gain((N_EVEN, CMLP_W)),
        'cmlp_ln_b': nrm((N_EVEN, CMLP_W), 0.02),
        'cmlp_ws': nrm((N_EVEN, CMLP_GROUPS, CMLP_CHUNK, CMLP_CHUNK), CMLP_CHUNK ** -0.5),
        'cmlp_bs': 1.0 + nrm((N_EVEN, CMLP_GROUPS, CMLP_CHUNK), 0.1),
        'even_w_out': nrm((N_EVEN, RET_W + CMLP_W, D_MODEL), (RET_W + CMLP_W) ** -0.5),
        'odd_w_in': nrm((N_ODD, D_MODEL, ODD_IN), D_MODEL ** -0.5),
        'fox_b_f': 3.0 + nrm((N_ODD, FOX_HEADS)),
        'fox_qn_g': gain((N_ODD, FOX_HD)),
        'fox_kn_g': gain((N_ODD, FOX_HD)),
        'odd_w_out': nrm((N_ODD, FOX_W, D_MODEL), FOX_W ** -0.5),
        'mem_src_g': gain((DEPTH, D_MODEL)),
        'mem_w_q': nrm((DEPTH, D_MODEL, MEM_W), D_MODEL ** -0.5),
        'mem_w_k': nrm((DEPTH, D_MODEL, MEM_W), D_MODEL ** -0.5),
        'mem_w_v': nrm((DEPTH, D_MODEL, MEM_W), D_MODEL ** -0.5),
        'mem_qn_g': gain((DEPTH, MEM_HD)),
        'mem_kn_g': gain((DEPTH, MEM_HD)),
        'mem_w_o': nrm((DEPTH, MEM_W, D_MODEL), MEM_W ** -0.5),
        'moe_w_rg': nrm((DEPTH, D_MODEL, G), D_MODEL ** -0.5),
        'moe_b_rg': nrm((DEPTH, G), 0.01),
        'moe_w_re': nrm((DEPTH, D_MODEL, G * E), D_MODEL ** -0.5),
        'moe_b_re': nrm((DEPTH, G * E), 0.01),
        'moe_w_gate': nrm((DEPTH, G, E, D_MODEL, F), D_MODEL ** -0.5),
        'moe_w_up': nrm((DEPTH, G, E, D_MODEL, F), D_MODEL ** -0.5),
        'moe_w_down': nrm((DEPTH, G, E, F, D_MODEL), F ** -0.5),
    }


def reference(x_prompt, x_sample, mem_prompt, state_ret, cache_fox_k, cache_fox_v, cache_fox_logf,
              cache_mem_k, cache_mem_v, norm_mix, norm_mem, norm_moe,
              even_w_in, ret_gn_g, cmlp_ln_g, cmlp_ln_b, cmlp_ws, cmlp_bs, even_w_out,
              odd_w_in, fox_b_f, fox_qn_g, fox_kn_g, odd_w_out,
              mem_src_g, mem_w_q, mem_w_k, mem_w_v, mem_qn_g, mem_kn_g, mem_w_o,
              moe_w_rg, moe_b_rg, moe_w_re, moe_b_re, moe_w_gate, moe_w_up, moe_w_down):
    xp, xs = x_prompt, x_sample
    B, T_p, _ = xp.shape
    T_s = xs.shape[1]
    P = cache_fox_k.shape[2]
    pos_p = jnp.arange(T_p)
    pos_s = P + jnp.arange(T_s)

    ret_S_p, ret_S_s, cmlp_v_s = [], [], []
    fk_p, fv_p, fl_p, fk_s, fv_s, fl_s = [], [], [], [], [], []
    mk_p, mv_p = [], []
    for i in range(DEPTH):
        j = i // 2
        hp = rms_norm(xp, norm_mix[i])
        hs = rms_norm(xs, norm_mix[i])
        if i % 2 == 0:
            prm = (even_w_in[j], ret_gn_g[j], cmlp_ln_g[j], cmlp_ln_b[j], cmlp_ws[j], cmlp_bs[j], even_w_out[j])
            S0 = jnp.zeros((B, RET_HEADS, RET_HD, RET_HD), xp.dtype)
            yp, Sp, _ = even_mixer(hp, pos_p, S0, *prm)
            ys, Ss, zvs = even_mixer(hs, pos_s, state_ret[j], *prm)
            ret_S_p.append(Sp)
            ret_S_s.append(Ss)
            cmlp_v_s.append(zvs)
        else:
            prm = (odd_w_in[j], fox_b_f[j], fox_qn_g[j], fox_kn_g[j])
            qp, kp, vp, lp = fox_project(hp, *prm)
            yp = fox_prompt(qp, kp, vp, lp) @ odd_w_out[j]
            qs, ks_, vs, ls = fox_project(hs, *prm)
            ys = fox_sample(qs, ks_, vs, ls, cache_fox_k[j], cache_fox_v[j], cache_fox_logf[j]) @ odd_w_out[j]
            fk_p.append(kp)
            fv_p.append(vp)
            fl_p.append(lp)
            fk_s.append(ks_)
            fv_s.append(vs)
            fl_s.append(ls)
        xp = xp + yp
        xs = xs + ys
        mkp, mvp = memory_kv(mem_prompt, mem_src_g[i], mem_w_k[i], mem_w_v[i], mem_kn_g[i])
        mk_p.append(mkp)
        mv_p.append(mvp)
        xp = xp + memory_attn(rms_norm(xp, norm_mem[i]), mkp, mvp, mem_w_q[i], mem_qn_g[i], mem_w_o[i])
        xs = xs + memory_attn(rms_norm(xs, norm_mem[i]), cache_mem_k[i], cache_mem_v[i], mem_w_q[i], mem_qn_g[i], mem_w_o[i])
        moe = (moe_w_rg[i], moe_b_rg[i], moe_w_re[i], moe_b_re[i], moe_w_gate[i], moe_w_up[i], moe_w_down[i])
        xp = xp + hier_moe(rms_norm(xp, norm_moe[i]), *moe)
        xs = xs + hier_moe(rms_norm(xs, norm_moe[i]), *moe)

    return (xp, xs,
            jnp.stack(ret_S_p), jnp.stack(ret_S_s), jnp.stack(cmlp_v_s),
            jnp.stack(fk_p), jnp.stack(fv_p), jnp.stack(fl_p),
            jnp.stack(fk_s), jnp.stack(fv_s), jnp.stack(fl_s),
            jnp.stack(mk_p), jnp.stack(mv_p))
```

```python
import functools

import jax
import jax.numpy as jnp
from jax import lax
from jax.experimental import pallas as pl
from jax.experimental.pallas import tpu as pltpu

F32 = jnp.float32
BF16 = jnp.bfloat16
EPS = 1e-6
ROPE_BASE = 10000.0
LANES = 128
VMEM_LIMIT_BYTES = 56 * 1024 * 1024


def _cparams(semantics):
    return pltpu.CompilerParams(dimension_semantics=semantics,
                                vmem_limit_bytes=VMEM_LIMIT_BYTES)


def _pick(n, pref):
    t = min(n, pref)
    while n % t:
        t //= 2
    assert t >= 1
    return t


def _rms(x, g):
    return x * lax.rsqrt(jnp.mean(x * x, axis=-1, keepdims=True) + EPS) * g


def _rmsnorm_kernel(x_ref, g_ref, o_ref):
    o_ref[...] = _rms(x_ref[...], g_ref[...]).astype(o_ref.dtype)


def rmsnorm_rows(x, g, out_dtype=BF16):
    n, d = x.shape
    tm = _pick(n, 512)
    return pl.pallas_call(
        _rmsnorm_kernel,
        out_shape=jax.ShapeDtypeStruct((n, d), out_dtype),
        grid=(n // tm,),
        in_specs=[pl.BlockSpec((tm, d), lambda i: (i, 0)),
                  pl.BlockSpec((1, d), lambda i: (0, 0))],
        out_specs=pl.BlockSpec((tm, d), lambda i: (i, 0)),
        compiler_params=_cparams(("parallel",)),
        name="rmsnorm_rows",
    )(x, g.reshape(1, d))


def _mm_kernel(*refs, n_extra, epilogue):
    a_ref, w_ref = refs[0], refs[1]
    extra = refs[2:2 + n_extra]
    outs = refs[2 + n_extra:]
    acc = jnp.dot(a_ref[...], w_ref[...], preferred_element_type=F32)
    epilogue(acc, extra, outs)


def _ep_store(acc, extra, outs):
    outs[0][...] = acc.astype(outs[0].dtype)


def _ep_residual(acc, extra, outs):
    outs[0][...] = extra[0][...] + acc


def _ep_logsig(acc, extra, outs):
    outs[0][...] = jax.nn.log_sigmoid(acc + extra[0][...])


def _ep_heads(acc, extra, outs, *, hd, norm, scale, kinds):
    tn = acc.shape[1]
    for hh in range(tn // hd):
        blk = acc[:, hh * hd:(hh + 1) * hd]
        if norm:
            blk = _rms(blk, extra[0][...])
        if scale != 1.0:
            blk = blk * scale
        for o_ref, kind in zip(outs, kinds):
            if kind == "tok":
                o_ref[:, hh * hd:(hh + 1) * hd] = blk.astype(o_ref.dtype)
            else:
                o_ref[0, hh] = blk.astype(o_ref.dtype)


def matmul(a, w, *, epilogue=_ep_store, extras=(), extra_specs=(), out_shapes, out_specs,
           tm, tn, name):
    m, k = a.shape
    k2, n = w.shape
    assert k == k2 and m % tm == 0 and n % tn == 0
    kern = functools.partial(_mm_kernel, n_extra=len(extras), epilogue=epilogue)
    return pl.pallas_call(
        kern,
        out_shape=out_shapes,
        grid=(m // tm, n // tn),
        in_specs=[pl.BlockSpec((tm, k), lambda i, j: (i, 0)),
                  pl.BlockSpec((k, tn), lambda i, j: (0, j))] + list(extra_specs),
        out_specs=out_specs,
        compiler_params=_cparams(("parallel", "parallel")),
        name=name,
    )(a, w, *extras)


def matmul_plain(a, w, out_dtype, name, residual=None):
    m, _ = a.shape
    n = w.shape[1]
    tm, tn = _pick(m, 1024), _pick(n, 512)
    tile = pl.BlockSpec((tm, tn), lambda i, j: (i, j))
    if residual is None:
        return matmul(a, w, out_shapes=jax.ShapeDtypeStruct((m, n), out_dtype), out_specs=tile,
                      tm=tm, tn=tn, name=name)
    return matmul(a, w, epilogue=_ep_residual, extras=(residual,), extra_specs=(tile,),
                  out_shapes=jax.ShapeDtypeStruct((m, n), F32), out_specs=tile,
                  tm=tm, tn=tn, name=name)


def _mm2_residual_kernel(a1_ref, w1_ref, a2_ref, w2_ref, res_ref, o_ref):
    o_ref[...] = (res_ref[...] + jnp.dot(a1_ref[...], w1_ref[...], preferred_element_type=F32)
                  + jnp.dot(a2_ref[...], w2_ref[...], preferred_element_type=F32))


def matmul2_residual(a1, w1, a2, w2, residual, name):
    m, k1 = a1.shape
    k2 = a2.shape[1]
    n = w1.shape[1]
    tm, tn = _pick(m, 1024), _pick(n, 512)
    tile = pl.BlockSpec((tm, tn), lambda i, j: (i, j))
    return pl.pallas_call(
        _mm2_residual_kernel,
        out_shape=jax.ShapeDtypeStruct((m, n), F32),
        grid=(m // tm, n // tn),
        in_specs=[pl.BlockSpec((tm, k1), lambda i, j: (i, 0)),
                  pl.BlockSpec((k1, tn), lambda i, j: (0, j)),
                  pl.BlockSpec((tm, k2), lambda i, j: (i, 0)),
                  pl.BlockSpec((k2, tn), lambda i, j: (0, j)),
                  tile],
        out_specs=tile,
        compiler_params=_cparams(("parallel", "parallel")),
        name=name,
    )(a1, w1, a2, w2, residual)


def _ret_kernel(lg_ref, gl_ref, q_ref, k_ref, v_ref, gate_ref, cos_ref, sin_ref, gn_ref, *rest,
                L, hd, has_s0):
    if has_s0:
        s0_ref, y_ref, sout_ref, s_scr = rest
    else:
        y_ref, sout_ref, s_scr = rest
    h = pl.program_id(1)
    c = pl.program_id(2)
    lg = lg_ref[h]

    @pl.when(c == 0)
    def _():
        if has_s0:
            s_scr[...] = s0_ref[0, 0]
        else:
            s_scr[...] = jnp.zeros_like(s_scr)

    half = hd // 2
    cos = cos_ref[...]
    sin = sin_ref[...]

    def rope(x):
        x1, x2 = x[:, :half], x[:, half:]
        return jnp.concatenate([x1 * cos - x2 * sin, x1 * sin + x2 * cos], axis=-1)

    q = rope(q_ref[...])
    k = rope(k_ref[...]) * (hd ** -0.5)
    vb = v_ref[...].astype(BF16)
    n_col = lax.broadcasted_iota(jnp.int32, (L, 1), 0).astype(F32)
    diff = (lax.broadcasted_iota(jnp.int32, (L, L), 0)
            - lax.broadcasted_iota(jnp.int32, (L, L), 1)).astype(F32)
    d_in = jnp.where(diff >= 0, jnp.exp(diff * lg), 0.0)
    qb = q.astype(BF16)
    s = lax.dot_general(qb, k.astype(BF16), (((1,), (1,)), ((), ())),
                        preferred_element_type=F32) * d_in
    inner = jnp.dot(s.astype(BF16), vb, preferred_element_type=F32)
    s_old = s_scr[...]
    cross = jnp.dot(qb, s_old.astype(BF16), preferred_element_type=F32) * jnp.exp((n_col + 1.0) * lg)
    kd = (k * jnp.exp((L - 1.0 - n_col) * lg)).astype(BF16)
    s_new = gl_ref[h] * s_old + lax.dot_general(kd, vb, (((0,), (0,)), ((), ())),
                                                preferred_element_type=F32)
    s_scr[...] = s_new
    o = inner + cross
    xc = o - jnp.mean(o, axis=-1, keepdims=True)
    on = xc * lax.rsqrt(jnp.mean(xc * xc, axis=-1, keepdims=True) + EPS) * gn_ref[...]
    y_ref[...] = (jax.nn.silu(gate_ref[...]) * on).astype(y_ref.dtype)

    @pl.when(c == pl.num_programs(2) - 1)
    def _():
        sout_ref[0, 0] = s_new


def retention(proj, cos, sin, gn_g, s0, *, B, T, H, hd):
    L = _pick(T, 256)
    nc = T // L
    lg = jnp.log1p(-jnp.exp2(-5.0 - jnp.arange(H, dtype=F32)))
    gl = jnp.exp(L * lg)
    has_s0 = s0 is not None

    def col(sec):
        return pl.BlockSpec((L, hd), lambda b, h, c: (b * nc + c, sec * H + h))

    smem = pl.BlockSpec(memory_space=pltpu.SMEM)
    in_specs = [smem, smem, col(0), col(1), col(2), col(3),
                pl.BlockSpec((L, hd // 2), lambda b, h, c: (c, 0)),
                pl.BlockSpec((L, hd // 2), lambda b, h, c: (c, 0)),
                pl.BlockSpec((1, hd), lambda b, h, c: (0, h))]
    args = [lg, gl, proj, proj, proj, proj, cos, sin, gn_g.reshape(1, H * hd)]
    state_spec = pl.BlockSpec((1, 1, hd, hd), lambda b, h, c: (b, h, 0, 0))
    if has_s0:
        in_specs.append(state_spec)
        args.append(s0)
    return pl.pallas_call(
        functools.partial(_ret_kernel, L=L, hd=hd, has_s0=has_s0),
        out_shape=(jax.ShapeDtypeStruct((B * T, H * hd), BF16),
                   jax.ShapeDtypeStruct((B, H, hd, hd), F32)),
        grid=(B, H, nc),
        in_specs=in_specs,
        out_specs=(pl.BlockSpec((L, hd), lambda b, h, c: (b * nc + c, h)), state_spec),
        scratch_shapes=[pltpu.VMEM((hd, hd), F32)],
        compiler_params=_cparams(("parallel", "parallel", "arbitrary")),
        name="retention",
    )(*args)


def _cmlp_kernel(u_ref, vb_ref, lng_ref, lnb_ref, ws_ref, bst_ref, y_ref, *zv_out, G, gd, L):
    zu = jax.nn.gelu(u_ref[...])
    gv = jax.nn.gelu(vb_ref[...])
    xc = gv - jnp.mean(gv, axis=-1, keepdims=True)
    zv = xc * lax.rsqrt(jnp.mean(xc * xc, axis=-1, keepdims=True) + EPS) * lng_ref[...] + lnb_ref[...]
    if zv_out:
        zv_out[0][...] = zv
    keep = (lax.broadcasted_iota(jnp.int32, (L, L), 0) >= lax.broadcasted_iota(jnp.int32, (L, L), 1))
    for g in range(G):
        w = jnp.where(keep, ws_ref[g, :L, :L], 0.0).astype(BF16)
        mixed = jnp.dot(w, zv[:, g * gd:(g + 1) * gd].astype(BF16), preferred_element_type=F32)
        mixed = mixed + bst_ref[:, g:g + 1]
        y_ref[:, g * gd:(g + 1) * gd] = (zu[:, g * gd:(g + 1) * gd] * mixed).astype(y_ref.dtype)


def cmlp(proj, ln_g, ln_b, ws, bs, *, B, T, u_col, want_zv):
    G, chunk, _ = ws.shape
    W = ln_g.shape[0]
    gd = W // G
    L = min(T, chunk)
    nc = T // L
    out_shape = [jax.ShapeDtypeStruct((B * T, W), BF16)]
    out_specs = [pl.BlockSpec((L, W), lambda b, c: (b * nc + c, 0))]
    if want_zv:
        out_shape.append(jax.ShapeDtypeStruct((B * T, W), F32))
        out_specs.append(pl.BlockSpec((L, W), lambda b, c: (b * nc + c, 0)))
    res = pl.pallas_call(
        functools.partial(_cmlp_kernel, G=G, gd=gd, L=L),
        out_shape=tuple(out_shape),
        grid=(B, nc),
        in_specs=[pl.BlockSpec((L, W), lambda b, c: (b * nc + c, u_col)),
                  pl.BlockSpec((L, W), lambda b, c: (b * nc + c, u_col + 1)),
                  pl.BlockSpec((1, W), lambda b, c: (0, 0)),
                  pl.BlockSpec((1, W), lambda b, c: (0, 0)),
                  pl.BlockSpec((G, chunk, chunk), lambda b, c: (0, 0, 0)),
                  pl.BlockSpec((L, G), lambda b, c: (0, 0))],
        out_specs=tuple(out_specs),
        compiler_params=_cparams(("parallel", "parallel")),
        name="cmlp",
    )(proj, proj, ln_g.reshape(1, W), ln_b.reshape(1, W), ws, bs[:, :L].T)
    return res if want_zv else (res[0], None)


def _mem_kernel(x_ref, g_ref, wq_ref, qn_ref, mk_ref, mv_ref, wo_ref, o_ref, *, heads, hd):
    x = x_ref[...]
    h = _rms(x, g_ref[...]).astype(BF16)
    q = jnp.dot(h, wq_ref[...], preferred_element_type=F32)
    mk = mk_ref[0]
    mv = mv_ref[0]
    outs = []
    for hh in range(heads):
        sl = slice(hh * hd, (hh + 1) * hd)
        qh = _rms(q[:, sl], qn_ref[...]).astype(BF16)
        s = lax.dot_general(qh, mk[:, sl].astype(BF16), (((1,), (1,)), ((), ())),
                            preferred_element_type=F32) * (hd ** -0.5)
        e = jnp.exp(s - jnp.max(s, axis=-1, keepdims=True))
        p = (e / jnp.sum(e, axis=-1, keepdims=True)).astype(BF16)
        outs.append(jnp.dot(p, mv[:, sl].astype(BF16), preferred_element_type=F32))
    o = jnp.concatenate(outs, axis=-1).astype(BF16)
    o_ref[...] = x + jnp.dot(o, wo_ref[...], preferred_element_type=F32)


def mem_attn(x, g, wq, qn_g, mk, mv, wo, *, B, T, heads):
    n, d = x.shape
    mw = wq.shape[1]
    hd = mw // heads
    mlen = mk.shape[1]
    tm = _pick(T, 256)
    tpb = T // tm
    return pl.pallas_call(
        functools.partial(_mem_kernel, heads=heads, hd=hd),
        out_shape=jax.ShapeDtypeStruct((n, d), F32),
        grid=(n // tm,),
        in_specs=[pl.BlockSpec((tm, d), lambda i: (i, 0)),
                  pl.BlockSpec((1, d), lambda i: (0, 0)),
                  pl.BlockSpec((d, mw), lambda i: (0, 0)),
                  pl.BlockSpec((1, hd), lambda i: (0, 0)),
                  pl.BlockSpec((1, mlen, mw), lambda i: (i // tpb, 0, 0)),
                  pl.BlockSpec((1, mlen, mw), lambda i: (i // tpb, 0, 0)),
                  pl.BlockSpec((mw, d), lambda i: (0, 0))],
        out_specs=pl.BlockSpec((tm, d), lambda i: (i, 0)),
        compiler_params=_cparams(("parallel",)),
        name="mem_attn",
    )(x, g.reshape(1, d), wq, qn_g.reshape(1, hd), mk, mv, wo)


def _router_kernel(x_ref, g_ref, w_ref, b_ref, ids_ref, gates_ref, *, G, E):
    h = _rms(x_ref[...], g_ref[...])
    logits = jnp.dot(h, w_ref[...], preferred_element_type=F32,
                     precision=lax.Precision.HIGHEST) + b_ref[...]
    lane = lax.broadcasted_iota(jnp.int32, logits.shape, 1)
    neg = -jnp.inf
    is_g = lane < G
    lgm = jnp.where(is_g, logits, neg)
    gmax = jnp.max(lgm, axis=-1, keepdims=True)
    gsel = jnp.min(jnp.where(lgm == gmax, lane, LANES), axis=-1, keepdims=True)
    p_grp = 1.0 / jnp.sum(jnp.where(is_g, jnp.exp(logits - gmax), 0.0), axis=-1, keepdims=True)
    lo = G + gsel * E
    lem = jnp.where((lane >= lo) & (lane < lo + E), logits, neg)
    v1 = jnp.max(lem, axis=-1, keepdims=True)
    i1 = jnp.min(jnp.where(lem == v1, lane, LANES), axis=-1, keepdims=True)
    lem2 = jnp.where(lane == i1, neg, lem)
    v2 = jnp.max(lem2, axis=-1, keepdims=True)
    i2 = jnp.min(jnp.where(lem2 == v2, lane, LANES), axis=-1, keepdims=True)
    e2 = jnp.exp(v2 - v1)
    den = 1.0 + e2
    ids_ref[...] = jnp.where(lane == 0, i1 - G, jnp.where(lane == 1, i2 - G, 0))
    gates_ref[...] = jnp.where(lane == 0, p_grp / den, jnp.where(lane == 1, p_grp * e2 / den, 0.0))


def moe_router(x, g, w_rg, b_rg, w_re, b_re):
    n, d = x.shape
    G = w_rg.shape[1]
    E = w_re.shape[1] // G
    assert G + G * E <= LANES
    pad = LANES - G - G * E
    w = jnp.concatenate([w_rg, w_re, jnp.zeros((d, pad), F32)], axis=1)
    b = jnp.concatenate([b_rg, b_re, jnp.zeros((pad,), F32)]).reshape(1, LANES)
    tm = _pick(n, 256)
    row = pl.BlockSpec((tm, LANES), lambda i: (i, 0))
    return pl.pallas_call(
        functools.partial(_router_kernel, G=G, E=E),
        out_shape=(jax.ShapeDtypeStruct((n, LANES), jnp.int32),
                   jax.ShapeDtypeStruct((n, LANES), F32)),
        grid=(n // tm,),
        in_specs=[pl.BlockSpec((tm, d), lambda i: (i, 0)),
                  pl.BlockSpec((1, d), lambda i: (0, 0)),
                  pl.BlockSpec((d, LANES), lambda i: (0, 0)),
                  pl.BlockSpec((1, LANES), lambda i: (0, 0))],
        out_specs=(row, row),
        compiler_params=_cparams(("parallel",)),
        name="moe_router",
    )(x, g.reshape(1, d), w, b)


def moe_dispatch(ids, n_experts, tm):
    n = ids.shape[0]
    flat = ids[:, :2].reshape(-1)
    onehot = (flat[:, None] == jnp.arange(n_experts, dtype=jnp.int32)[None, :]).astype(jnp.int32)
    csum = jnp.cumsum(onehot, axis=0)
    counts = csum[-1]
    rank = jnp.take_along_axis(csum, flat[:, None], axis=1)[:, 0] - 1
    padded = ((counts + tm - 1) // tm) * tm
    ends = jnp.cumsum(padded)
    starts = ends - padded
    pos = starts[flat] + rank
    max_tiles = (2 * n) // tm + n_experts
    tile_start = jnp.arange(max_tiles, dtype=jnp.int32) * tm
    tile_expert = jnp.minimum(jnp.searchsorted(ends, tile_start, side="right"),
                              n_experts - 1).astype(jnp.int32)
    order = jnp.argsort(flat, stable=True).astype(jnp.int32)
    raw_starts = jnp.cumsum(counts) - counts
    row_e = jnp.repeat(tile_expert, tm)
    k_in_e = jnp.arange(max_tiles * tm, dtype=jnp.int32) - starts[row_e]
    src = jnp.minimum(raw_starts[row_e] + k_in_e, 2 * n - 1)
    row_token = jnp.where(k_in_e < counts[row_e], order[src] // 2, 0).astype(jnp.int32)
    n_tiles = (ends[-1] // tm).astype(jnp.int32).reshape(1)
    return row_token, pos.astype(jnp.int32), tile_expert, n_tiles


def _row_gather_start(src_hbm, dst, idx_ref, base, rows, sem):
    def body(r, carry):
        pltpu.make_async_copy(src_hbm.at[pl.ds(idx_ref[base + r], 1)], dst.at[pl.ds(r, 1)], sem).start()
        return carry
    lax.fori_loop(0, rows, body, 0)


def _row_gather_wait(src_hbm, dst, rows, sem):
    def body(r, carry):
        pltpu.make_async_copy(src_hbm.at[pl.ds(0, 1)], dst.at[pl.ds(r, 1)], sem).wait()
        return carry
    lax.fori_loop(0, rows, body, 0)


def _ffn_kernel(te_ref, rt_ref, nt_ref, x_hbm, g_ref, wg_ref, wu_ref, wd_ref, y_ref, xbuf, sem, *, tm):
    del te_ref
    i = pl.program_id(0)
    nt = nt_ref[0]
    slot = i % 2

    @pl.when(i == 0)
    def _():
        _row_gather_start(x_hbm, xbuf.at[0], rt_ref, 0, tm, sem.at[0])

    @pl.when(i + 1 < nt)
    def _():
        _row_gather_start(x_hbm, xbuf.at[1 - slot], rt_ref, (i + 1) * tm, tm, sem.at[1 - slot])

    @pl.when(i < nt)
    def _():
        _row_gather_wait(x_hbm, xbuf.at[slot], tm, sem.at[slot])
        h = _rms(xbuf[slot], g_ref[...]).astype(BF16)
        a = (jax.nn.silu(jnp.dot(h, wg_ref[0], preferred_element_type=F32))
             * jnp.dot(h, wu_ref[0], preferred_element_type=F32))
        y_ref[...] = jnp.dot(a.astype(BF16), wd_ref[0], preferred_element_type=F32)

    @pl.when(i >= nt)
    def _():
        y_ref[...] = jnp.zeros_like(y_ref)


def moe_ffn(x, g, wg, wu, wd, row_token, tile_expert, n_tiles, *, tm):
    n, d = x.shape
    ne, _, f = wg.shape
    max_tiles = tile_expert.shape[0]
    grid_spec = pltpu.PrefetchScalarGridSpec(
        num_scalar_prefetch=3,
        grid=(max_tiles,),
        in_specs=[pl.BlockSpec(memory_space=pl.ANY),
                  pl.BlockSpec((1, d), lambda i, te, rt, nt: (0, 0)),
                  pl.BlockSpec((1, d, f), lambda i, te, rt, nt: (te[i], 0, 0)),
                  pl.BlockSpec((1, d, f), lambda i, te, rt, nt: (te[i], 0, 0)),
                  pl.BlockSpec((1, f, d), lambda i, te, rt, nt: (te[i], 0, 0))],
        out_specs=pl.BlockSpec((tm, d), lambda i, te, rt, nt: (i, 0)),
        scratch_shapes=[pltpu.VMEM((2, tm, d), F32), pltpu.SemaphoreType.DMA((2,))],
    )
    return pl.pallas_call(
        functools.partial(_ffn_kernel, tm=tm),
        out_shape=jax.ShapeDtypeStruct((max_tiles * tm, d), F32),
        grid_spec=grid_spec,
        compiler_params=_cparams(("arbitrary",)),
        name="moe_ffn",
    )(tile_expert, row_token, n_tiles, x, g.reshape(1, d), wg, wu, wd)


def _combine_kernel(pos_ref, x_ref, gates_ref, y_hbm, o_ref, ybuf, sem, *, tc):
    i = pl.program_id(0)
    nsteps = pl.num_programs(0)
    slot = i % 2

    @pl.when(i == 0)
    def _():
        _row_gather_start(y_hbm, ybuf.at[0], pos_ref, 0, 2 * tc, sem.at[0])

    @pl.when(i + 1 < nsteps)
    def _():
        _row_gather_start(y_hbm, ybuf.at[1 - slot], pos_ref, (i + 1) * 2 * tc, 2 * tc, sem.at[1 - slot])

    _row_gather_wait(y_hbm, ybuf.at[slot], 2 * tc, sem.at[slot])
    gates = gates_ref[...]
    o_ref[...] = (x_ref[...] + gates[:, 0:1] * ybuf[slot, pl.ds(0, tc)]
                  + gates[:, 1:2] * ybuf[slot, pl.ds(tc, tc)])


def moe_combine(x, gates, y_rows, pos, *, tc):
    n, d = x.shape
    nt = n // tc
    pos_tiled = pos.reshape(nt, tc, 2).transpose(0, 2, 1).reshape(-1)
    grid_spec = pltpu.PrefetchScalarGridSpec(
        num_scalar_prefetch=1,
        grid=(nt,),
        in_specs=[pl.BlockSpec((tc, d), lambda i, p: (i, 0)),
                  pl.BlockSpec((tc, LANES), lambda i, p: (i, 0)),
                  pl.BlockSpec(memory_space=pl.ANY)],
        out_specs=pl.BlockSpec((tc, d), lambda i, p: (i, 0)),
        scratch_shapes=[pltpu.VMEM((2, 2 * tc, d), F32), pltpu.SemaphoreType.DMA((2,))],
    )
    return pl.pallas_call(
        functools.partial(_combine_kernel, tc=tc),
        out_shape=jax.ShapeDtypeStruct((n, d), F32),
        grid_spec=grid_spec,
        compiler_params=_cparams(("arbitrary",)),
        name="moe_combine",
    )(pos_tiled, x, gates, y_rows)


def hier_moe(x, g, w_rg, b_rg, w_re, b_re, wg, wu, wd):
    n = x.shape[0]
    ne = wg.shape[0]
    tm = _pick(n, 256)
    ids, gates = moe_router(x, g, w_rg, b_rg, w_re, b_re)
    row_token, pos, tile_expert, n_tiles = moe_dispatch(ids, ne, tm)
    y_rows = moe_ffn(x, g, wg, wu, wd, row_token, tile_expert, n_tiles, tm=tm)
    return moe_combine(x, gates, y_rows, pos, tc=_pick(n, 128))


def _cumsum_kernel(x_ref, c_ref, ct_ref, carry, *, L):
    @pl.when(pl.program_id(1) == 0)
    def _():
        carry[...] = jnp.zeros_like(carry)

    tri = (lax.broadcasted_iota(jnp.int32, (L, L), 0)
           >= lax.broadcasted_iota(jnp.int32, (L, L), 1)).astype(F32)
    c = jnp.dot(tri, x_ref[0], preferred_element_type=F32,
                precision=lax.Precision.HIGHEST) + carry[...]
    c_ref[0] = c
    ct_ref[0] = c.T
    carry[...] = c[L - 1:L, :]


def cumsum_time(x):
    B, T, w = x.shape
    L = _pick(T, 256)
    return pl.pallas_call(
        functools.partial(_cumsum_kernel, L=L),
        out_shape=(jax.ShapeDtypeStruct((B, T, w), F32), jax.ShapeDtypeStruct((B, w, T), F32)),
        grid=(B, T // L),
        in_specs=[pl.BlockSpec((1, L, w), lambda b, t: (b, t, 0))],
        out_specs=(pl.BlockSpec((1, L, w), lambda b, t: (b, t, 0)),
                   pl.BlockSpec((1, w, L), lambda b, t: (b, 0, t))),
        scratch_shapes=[pltpu.VMEM((1, w), F32)],
        compiler_params=_cparams(("parallel", "arbitrary")),
        name="cumsum_time",
    )(x)


def _softmax_step(t, cq, v, m_old, l_old, acc_old):
    m_new = jnp.maximum(m_old, jnp.max(t, axis=-1, keepdims=True) + cq)
    alpha = jnp.exp(m_old - m_new)
    p = jnp.exp(t + (cq - m_new))
    l_new = alpha * l_old + jnp.sum(p, axis=-1, keepdims=True)
    acc_new = alpha * acc_old + jnp.dot(p.astype(BF16), v, preferred_element_type=F32)
    return m_new, l_new, acc_new


def _fox_prompt_kernel(q_ref, k_ref, v_ref, cq_ref, ck_ref, o_ref, m_scr, l_scr, acc_scr,
                       *, HG, tq, tk, hd):
    hg = pl.program_id(1)
    qi = pl.program_id(2)
    ki = pl.program_id(3)

    @pl.when(ki == 0)
    def _():
        m_scr[...] = jnp.full_like(m_scr, -jnp.inf)
        l_scr[...] = jnp.zeros_like(l_scr)
        acc_scr[...] = jnp.zeros_like(acc_scr)

    def run(masked):
        cq_all = cq_ref[...]
        lane = lax.broadcasted_iota(jnp.int32, cq_all.shape, 1)
        if masked:
            keep = (lax.broadcasted_iota(jnp.int32, (tq, tk), 0) + qi * tq
                    >= lax.broadcasted_iota(jnp.int32, (tq, tk), 1) + ki * tk)

        def head(hh, carry):
            s = lax.dot_general(q_ref[0, hh], k_ref[0, hh], (((1,), (1,)), ((), ())),
                                preferred_element_type=F32)
            t = s - ck_ref[0, pl.ds(hh, 1), :]
            if masked:
                t = jnp.where(keep, t, -jnp.inf)
            cq = jnp.sum(jnp.where(lane == hg * HG + hh, cq_all, 0.0), axis=-1, keepdims=True)
            m_new, l_new, acc_new = _softmax_step(t, cq, v_ref[0, hh], m_scr[hh], l_scr[hh], acc_scr[hh])
            m_scr[hh] = m_new
            l_scr[hh] = l_new
            acc_scr[hh] = acc_new
            return carry

        lax.fori_loop(0, HG, head, 0)

    @pl.when(ki < qi)
    def _():
        run(False)

    @pl.when(ki == qi)
    def _():
        run(True)
        for hh in range(HG):
            o_ref[:, hh * hd:(hh + 1) * hd] = (acc_scr[hh] / l_scr[hh]).astype(o_ref.dtype)


def fox_prompt(q_hm, k_hm, v_hm, c_tok, c_t, *, B, T, H, hd):
    HG = _pick(H, 16)
    tq = tk = _pick(T, 512)
    nq = T // tq
    kv_spec = pl.BlockSpec((1, HG, tk, hd), lambda b, g, qi, ki: (b, g, jnp.minimum(ki, qi), 0))
    return pl.pallas_call(
        functools.partial(_fox_prompt_kernel, HG=HG, tq=tq, tk=tk, hd=hd),
        out_shape=jax.ShapeDtypeStruct((B * T, H * hd), BF16),
        grid=(B, H // HG, nq, nq),
        in_specs=[pl.BlockSpec((1, HG, tq, hd), lambda b, g, qi, ki: (b, g, qi, 0)),
                  kv_spec, kv_spec,
                  pl.BlockSpec((tq, LANES), lambda b, g, qi, ki: (b * nq + qi, 0)),
                  pl.BlockSpec((1, HG, tk), lambda b, g, qi, ki: (b, g, jnp.minimum(ki, qi)))],
        out_specs=pl.BlockSpec((tq, HG * hd), lambda b, g, qi, ki: (b * nq + qi, g)),
        scratch_shapes=[pltpu.VMEM((HG, tq, 1), F32), pltpu.VMEM((HG, tq, 1), F32),
                        pltpu.VMEM((HG, tq, hd), F32)],
        compiler_params=_cparams(("parallel", "parallel", "parallel", "arbitrary")),
        name="fox_prompt",
    )(q_hm, k_hm, v_hm, c_tok, c_t)


def _fox_sample_kernel(q_ref, kn_ref, vn_ref, kp_ref, vp_ref, cq_ref, cn_ref, cp_ref, o_ref,
                       m_scr, l_scr, acc_scr, *, H, hd, Ts, tk):
    j = pl.program_id(1)

    @pl.when(j == 0)
    def _():
        keep = (lax.broadcasted_iota(jnp.int32, (Ts, Ts), 0) >= lax.broadcasted_iota(jnp.int32, (Ts, Ts), 1))
        for hh in range(H):
            sl = slice(hh * hd, (hh + 1) * hd)
            s = lax.dot_general(q_ref[:, sl], kn_ref[:, sl].astype(BF16), (((1,), (1,)), ((), ())),
                                preferred_element_type=F32)
            t = jnp.where(keep, s - cn_ref[0, hh:hh + 1, :Ts], -jnp.inf)
            m_new, l_new, acc_new = _softmax_step(
                t, cq_ref[0, :, hh:hh + 1], vn_ref[:, sl].astype(BF16),
                jnp.full((Ts, 1), -jnp.inf, F32), jnp.zeros((Ts, 1), F32), jnp.zeros((Ts, hd), F32))
            m_scr[hh] = m_new
            l_scr[hh] = l_new
            acc_scr[:, sl] = acc_new

    @pl.when(j > 0)
    def _():
        for hh in range(H):
            sl = slice(hh * hd, (hh + 1) * hd)
            s = lax.dot_general(q_ref[:, sl], kp_ref[:, sl].astype(BF16), (((1,), (1,)), ((), ())),
                                preferred_element_type=F32)
            t = s - cp_ref[0, hh:hh + 1, :]
            m_new, l_new, acc_new = _softmax_step(
                t, cq_ref[0, :, hh:hh + 1], vp_ref[:, sl].astype(BF16), m_scr[hh], l_scr[hh], acc_scr[:, sl])
            m_scr[hh] = m_new
            l_scr[hh] = l_new
            acc_scr[:, sl] = acc_new

    @pl.when(j == pl.num_programs(1) - 1)
    def _():
        for hh in range(H):
            sl = slice(hh * hd, (hh + 1) * hd)
            o_ref[:, sl] = (acc_scr[:, sl] / l_scr[hh]).astype(o_ref.dtype)


def fox_sample(q, k_new, v_new, k_past, v_past, c_tok, c_t, *, B, Ts, P, H, hd):
    w = H * hd
    tk = _pick(P, 256)
    npk = P // tk
    assert P % LANES == 0 and Ts <= LANES and P % Ts == 0
    past = pl.BlockSpec((tk, w), lambda b, j: (b * npk + jnp.maximum(j - 1, 0), 0))
    new = pl.BlockSpec((Ts, w), lambda b, j: (b, 0))
    return pl.pallas_call(
        functools.partial(_fox_sample_kernel, H=H, hd=hd, Ts=Ts, tk=tk),
        out_shape=jax.ShapeDtypeStruct((B * Ts, w), BF16),
        grid=(B, npk + 1),
        in_specs=[new, new, new, past, past,
                  pl.BlockSpec((1, Ts, LANES), lambda b, j: (b, P // Ts, 0)),
                  pl.BlockSpec((1, LANES, LANES), lambda b, j: (b, 0, P // LANES)),
                  pl.BlockSpec((1, LANES, tk), lambda b, j: (b, 0, jnp.maximum(j - 1, 0)))],
        out_specs=new,
        scratch_shapes=[pltpu.VMEM((H, Ts, 1), F32), pltpu.VMEM((H, Ts, 1), F32),
                        pltpu.VMEM((Ts, w), F32)],
        compiler_params=_cparams(("parallel", "arbitrary")),
        name="fox_sample",
    )(q, k_new, v_new, k_past, v_past, c_tok, c_t, c_t)


def fox_project(h, wq, wk, wv, wf, b_f, qn_g, kn_g, *, B, T, H, hd, head_major):
    m, _ = h.shape
    w = H * hd
    tm = _pick(T, 1024) if head_major else _pick(m, 1024)
    tn = _pick(w, 512)
    tpb = T // tm if head_major else 1
    tok = pl.BlockSpec((tm, tn), lambda i, j: (i, j))
    hm = pl.BlockSpec((1, tn // hd, tm, hd), lambda i, j: (i // tpb, j, i % tpb, 0))
    gain = pl.BlockSpec((1, hd), lambda i, j: (0, 0))
    tok_f32 = jax.ShapeDtypeStruct((m, w), F32)
    tok_bf16 = jax.ShapeDtypeStruct((m, w), BF16)
    hm_bf16 = jax.ShapeDtypeStruct((B, H, T, hd), BF16)

    def proj(wmat, g, norm, scale, shapes, specs, kinds, name):
        ep = functools.partial(_ep_heads, hd=hd, norm=norm, scale=scale, kinds=kinds)
        extras = (g.reshape(1, hd),) if norm else ()
        especs = (gain,) if norm else ()
        return matmul(h, wmat, epilogue=ep, extras=extras, extra_specs=especs,
                      out_shapes=shapes, out_specs=specs, tm=tm, tn=tn, name=name)

    if head_major:
        (q,) = proj(wq, qn_g, True, hd ** -0.5, (hm_bf16,), (hm,), ("head",), "fox_q")
        k, k_hm = proj(wk, kn_g, True, 1.0, (tok_f32, hm_bf16), (tok, hm), ("tok", "head"), "fox_k")
        v, v_hm = proj(wv, None, False, 1.0, (tok_f32, hm_bf16), (tok, hm), ("tok", "head"), "fox_v")
    else:
        (q,) = proj(wq, qn_g, True, hd ** -0.5, (tok_bf16,), (tok,), ("tok",), "fox_q")
        (k,) = proj(wk, kn_g, True, 1.0, (tok_f32,), (tok,), ("tok",), "fox_k")
        (v,) = proj(wv, None, False, 1.0, (tok_f32,), (tok,), ("tok",), "fox_v")
        k_hm = v_hm = None
    tmf = _pick(m, 1024)
    logf = matmul(h, wf, epilogue=_ep_logsig, extras=(b_f,),
                  extra_specs=(pl.BlockSpec((1, LANES), lambda i, j: (0, 0)),),
                  out_shapes=jax.ShapeDtypeStruct((m, LANES), F32),
                  out_specs=pl.BlockSpec((tmf, LANES), lambda i, j: (i, 0)),
                  tm=tmf, tn=LANES, name="fox_logf")
    return q, k, v, logf, k_hm, v_hm


def _stack(parts):
    return parts[0][None] if len(parts) == 1 else jnp.stack(parts)


def _rope_tables(pos, hd):
    half = hd // 2
    inv = ROPE_BASE ** (-jnp.arange(half, dtype=F32) / half)
    ang = pos.astype(F32)[:, None] * inv[None, :]
    return jnp.cos(ang), jnp.sin(ang)


def kernel(x_prompt, x_sample, mem_prompt, state_ret, cache_fox_k, cache_fox_v, cache_fox_logf, cache_mem_k, cache_mem_v, norm_mix, norm_mem, norm_moe, even_w_in, ret_gn_g, cmlp_ln_g, cmlp_ln_b, cmlp_ws, cmlp_bs, even_w_out, odd_w_in, fox_b_f, fox_qn_g, fox_kn_g, odd_w_out, mem_src_g, mem_w_q, mem_w_k, mem_w_v, mem_qn_g, mem_kn_g, mem_w_o, moe_w_rg, moe_b_rg, moe_w_re, moe_b_re, moe_w_gate, moe_w_up, moe_w_down):
    Bp, Tp, D = x_prompt.shape
    Bs, Ts, _ = x_sample.shape
    depth = norm_mix.shape[0]
    P = cache_fox_k.shape[2]
    RH, RHD = state_ret.shape[2], state_ret.shape[3]
    RW = RH * RHD
    CW = cmlp_ln_g.shape[1]
    FH, FHD = cache_fox_k.shape[3], cache_fox_k.shape[4]
    FW = FH * FHD
    MH, MHD = cache_mem_k.shape[3], cache_mem_k.shape[4]
    MLEN = mem_prompt.shape[1]
    NG, NE, _, DF = moe_w_gate.shape[1:]
    assert (4 * RW) % CW == 0 and FHD == LANES and FH <= LANES

    xp = x_prompt.reshape(Bp * Tp, D)
    xs = x_sample.reshape(Bs * Ts, D)
    mem2d = mem_prompt.reshape(Bp * MLEN, D)
    cos_p, sin_p = _rope_tables(jnp.arange(Tp), RHD)
    cos_s, sin_s = _rope_tables(P + jnp.arange(Ts), RHD)

    ret_S_p, ret_S_s, cmlp_v_s = [], [], []
    fk_p, fv_p, fl_p, fk_s, fv_s, fl_s = [], [], [], [], [], []
    mk_p, mv_p = [], []
    for i in range(depth):
        j = i // 2
        hp = rmsnorm_rows(xp, norm_mix[i])
        hs = rmsnorm_rows(xs, norm_mix[i])
        if i % 2 == 0:
            w_in = even_w_in[j].astype(BF16)
            w_out_a = even_w_out[j][:RW].astype(BF16)
            w_out_b = even_w_out[j][RW:].astype(BF16)

            def even(h, x, cos, sin, s0, B, T, want_zv):
                proj = matmul_plain(h, w_in, F32, "even_in")
                y_a, S = retention(proj, cos, sin, ret_gn_g[j], s0, B=B, T=T, H=RH, hd=RHD)
                y_b, zv = cmlp(proj, cmlp_ln_g[j], cmlp_ln_b[j], cmlp_ws[j], cmlp_bs[j],
                               B=B, T=T, u_col=4 * RW // CW, want_zv=want_zv)
                return matmul2_residual(y_a, w_out_a, y_b, w_out_b, x, "even_out"), S, zv

            xp, Sp, _ = even(hp, xp, cos_p, sin_p, None, Bp, Tp, False)
            xs, Ss, zvs = even(hs, xs, cos_s, sin_s, state_ret[j], Bs, Ts, True)
            ret_S_p.append(Sp)
            ret_S_s.append(Ss)
            cmlp_v_s.append(zvs.reshape(Bs, Ts, CW))
        else:
            w_in = odd_w_in[j]
            wq = w_in[:, :FW].astype(BF16)
            wk = w_in[:, FW:2 * FW].astype(BF16)
            wv = w_in[:, 2 * FW:3 * FW].astype(BF16)
            wf = jnp.pad(w_in[:, 3 * FW:], ((0, 0), (0, LANES - FH))).astype(BF16)
            b_f = jnp.pad(fox_b_f[j], (0, LANES - FH)).reshape(1, LANES)
            w_out = odd_w_out[j].astype(BF16)
            q, k, v, logf, k_hm, v_hm = fox_project(hp, wq, wk, wv, wf, b_f, fox_qn_g[j], fox_kn_g[j],
                                                    B=Bp, T=Tp, H=FH, hd=FHD, head_major=True)
            c_tok, c_t = cumsum_time(logf.reshape(Bp, Tp, LANES))
            o = fox_prompt(q, k_hm, v_hm, c_tok.reshape(Bp * Tp, LANES), c_t[:, :FH], B=Bp, T=Tp, H=FH, hd=FHD)
            xp = matmul_plain(o, w_out, F32, "odd_out", residual=xp)
            fk_p.append(k.reshape(Bp, Tp, FH, FHD))
            fv_p.append(v.reshape(Bp, Tp, FH, FHD))
            fl_p.append(logf[:, :FH].reshape(Bp, Tp, FH))
            q, k, v, logf, _, _ = fox_project(hs, wq, wk, wv, wf, b_f, fox_qn_g[j], fox_kn_g[j],
                                              B=Bs, T=Ts, H=FH, hd=FHD, head_major=False)
            t_pad = -(-(P + Ts) // 256) * 256
            seq = jnp.concatenate([jnp.pad(cache_fox_logf[j], ((0, 0), (0, 0), (0, LANES - FH))),
                                   logf.reshape(Bs, Ts, LANES),
                                   jnp.zeros((Bs, t_pad - P - Ts, LANES), F32)], axis=1)
            c_tok, c_t = cumsum_time(seq)
            o = fox_sample(q, k, v, cache_fox_k[j].reshape(Bs * P, FW), cache_fox_v[j].reshape(Bs * P, FW),
                           c_tok, c_t, B=Bs, Ts=Ts, P=P, H=FH, hd=FHD)
            xs = matmul_plain(o, w_out, F32, "odd_out", residual=xs)
            fk_s.append(k.reshape(Bs, Ts, FH, FHD))
            fv_s.append(v.reshape(Bs, Ts, FH, FHD))
            fl_s.append(logf[:, :FH].reshape(Bs, Ts, FH))
        m_n = rmsnorm_rows(mem2d, mem_src_g[i])
        mw = MH * MHD
        tmm, tnm = _pick(Bp * MLEN, 1024), _pick(mw, 512)
        tile = pl.BlockSpec((tmm, tnm), lambda a, b: (a, b))
        (mk,) = matmul(m_n, mem_w_k[i].astype(BF16),
                       epilogue=functools.partial(_ep_heads, hd=MHD, norm=True, scale=1.0, kinds=("tok",)),
                       extras=(mem_kn_g[i].reshape(1, MHD),),
                       extra_specs=(pl.BlockSpec((1, MHD), lambda a, b: (0, 0)),),
                       out_shapes=(jax.ShapeDtypeStruct((Bp * MLEN, mw), F32),), out_specs=(tile,),
                       tm=tmm, tn=tnm, name="mem_k")
        mv = matmul_plain(m_n, mem_w_v[i].astype(BF16), F32, "mem_v")
        mk_p.append(mk.reshape(Bp, MLEN, MH, MHD))
        mv_p.append(mv.reshape(Bp, MLEN, MH, MHD))
        wq_m = mem_w_q[i].astype(BF16)
        wo_m = mem_w_o[i].astype(BF16)
        xp = mem_attn(xp, norm_mem[i], wq_m, mem_qn_g[i], mk.reshape(Bp, MLEN, mw), mv.reshape(Bp, MLEN, mw),
                      wo_m, B=Bp, T=Tp, heads=MH)
        xs = mem_attn(xs, norm_mem[i], wq_m, mem_qn_g[i], cache_mem_k[i].reshape(Bs, MLEN, mw),
                      cache_mem_v[i].reshape(Bs, MLEN, mw), wo_m, B=Bs, T=Ts, heads=MH)
        wg = moe_w_gate[i].reshape(NG * NE, D, DF).astype(BF16)
        wu = moe_w_up[i].reshape(NG * NE, D, DF).astype(BF16)
        wd = moe_w_down[i].reshape(NG * NE, DF, D).astype(BF16)
        xp = hier_moe(xp, norm_moe[i], moe_w_rg[i], moe_b_rg[i], moe_w_re[i], moe_b_re[i], wg, wu, wd)
        xs = hier_moe(xs, norm_moe[i], moe_w_rg[i], moe_b_rg[i], moe_w_re[i], moe_b_re[i], wg, wu, wd)

    return (xp.reshape(Bp, Tp, D), xs.reshape(Bs, Ts, D),
            _stack(ret_S_p), _stack(ret_S_s), _stack(cmlp_v_s),
            _stack(fk_p), _stack(fv_p), _stack(fl_p),
            _stack(fk_s), _stack(fv_s), _stack(fl_s),
            _stack(mk_p), _stack(mv_p))
```

```python
import functools

import jax
import jax.numpy as jnp
from jax import lax
from jax.experimental import pallas as pl
from jax.experimental.pallas import tpu as pltpu

F32 = jnp.float32
BF16 = jnp.bfloat16
EPS = 1e-6
ROPE_BASE = 10000.0
LOG2E = 1.4426950408889634
LANES = 128
VMEM_LIMIT_BYTES = 56 * 1024 * 1024


def _cparams(semantics):
    return pltpu.CompilerParams(dimension_semantics=semantics,
                                vmem_limit_bytes=VMEM_LIMIT_BYTES)


def _pick(n, pref):
    t = min(n, pref)
    while n % t:
        t //= 2
    assert t >= 1
    return t


def _rms(x, g):
    return x * lax.rsqrt(jnp.mean(x * x, axis=-1, keepdims=True) + EPS) * g


def _rmsnorm_kernel(x_ref, g_ref, o_ref):
    o_ref[...] = _rms(x_ref[...], g_ref[...]).astype(o_ref.dtype)


def rmsnorm_rows(x, g, out_dtype=BF16):
    n, d = x.shape
    tm = _pick(n, 512)
    return pl.pallas_call(
        _rmsnorm_kernel,
        out_shape=jax.ShapeDtypeStruct((n, d), out_dtype),
        grid=(n // tm,),
        in_specs=[pl.BlockSpec((tm, d), lambda i: (i, 0)),
                  pl.BlockSpec((1, d), lambda i: (0, 0))],
        out_specs=pl.BlockSpec((tm, d), lambda i: (i, 0)),
        compiler_params=_cparams(("parallel",)),
        name="rmsnorm_rows",
    )(x, g.reshape(1, d))


def _mm_kernel(*refs, n_extra, epilogue):
    a_ref, w_ref = refs[0], refs[1]
    extra = refs[2:2 + n_extra]
    outs = refs[2 + n_extra:]
    acc = jnp.dot(a_ref[...], w_ref[...], preferred_element_type=F32)
    epilogue(acc, extra, outs)


def _ep_store(acc, extra, outs):
    outs[0][...] = acc.astype(outs[0].dtype)


def _ep_residual(acc, extra, outs):
    outs[0][...] = extra[0][...] + acc


def _ep_logsig(acc, extra, outs):
    outs[0][...] = jax.nn.log_sigmoid(acc + extra[0][...])


def _ep_heads(acc, extra, outs, *, hd, norm, scale, kinds):
    tn = acc.shape[1]
    for hh in range(tn // hd):
        blk = acc[:, hh * hd:(hh + 1) * hd]
        if norm:
            blk = _rms(blk, extra[0][...])
        if scale != 1.0:
            blk = blk * scale
        for o_ref, kind in zip(outs, kinds):
            if kind == "tok":
                o_ref[:, hh * hd:(hh + 1) * hd] = blk.astype(o_ref.dtype)
            else:
                o_ref[0, hh] = blk.astype(o_ref.dtype)


def matmul(a, w, *, epilogue=_ep_store, extras=(), extra_specs=(), out_shapes, out_specs,
           tm, tn, name):
    m, k = a.shape
    k2, n = w.shape
    assert k == k2 and m % tm == 0 and n % tn == 0
    kern = functools.partial(_mm_kernel, n_extra=len(extras), epilogue=epilogue)
    return pl.pallas_call(
        kern,
        out_shape=out_shapes,
        grid=(m // tm, n // tn),
        in_specs=[pl.BlockSpec((tm, k), lambda i, j: (i, 0)),
                  pl.BlockSpec((k, tn), lambda i, j: (0, j))] + list(extra_specs),
        out_specs=out_specs,
        compiler_params=_cparams(("parallel", "parallel")),
        name=name,
    )(a, w, *extras)


def matmul_plain(a, w, out_dtype, name, residual=None):
    m, _ = a.shape
    n = w.shape[1]
    tm, tn = _pick(m, 1024), _pick(n, 512)
    tile = pl.BlockSpec((tm, tn), lambda i, j: (i, j))
    if residual is None:
        return matmul(a, w, out_shapes=jax.ShapeDtypeStruct((m, n), out_dtype), out_specs=tile,
                      tm=tm, tn=tn, name=name)
    return matmul(a, w, epilogue=_ep_residual, extras=(residual,), extra_specs=(tile,),
                  out_shapes=jax.ShapeDtypeStruct((m, n), F32), out_specs=tile,
                  tm=tm, tn=tn, name=name)


def _mm2_residual_kernel(a1_ref, w1_ref, a2_ref, w2_ref, res_ref, o_ref):
    o_ref[...] = (res_ref[...] + jnp.dot(a1_ref[...], w1_ref[...], preferred_element_type=F32)
                  + jnp.dot(a2_ref[...], w2_ref[...], preferred_element_type=F32))


def matmul2_residual(a1, w1, a2, w2, residual, name):
    m, k1 = a1.shape
    k2 = a2.shape[1]
    n = w1.shape[1]
    tm, tn = _pick(m, 1024), _pick(n, 512)
    tile = pl.BlockSpec((tm, tn), lambda i, j: (i, j))
    return pl.pallas_call(
        _mm2_residual_kernel,
        out_shape=jax.ShapeDtypeStruct((m, n), F32),
        grid=(m // tm, n // tn),
        in_specs=[pl.BlockSpec((tm, k1), lambda i, j: (i, 0)),
                  pl.BlockSpec((k1, tn), lambda i, j: (0, j)),
                  pl.BlockSpec((tm, k2), lambda i, j: (i, 0)),
                  pl.BlockSpec((k2, tn), lambda i, j: (0, j)),
                  tile],
        out_specs=tile,
        compiler_params=_cparams(("parallel", "parallel")),
        name=name,
    )(a1, w1, a2, w2, residual)


def _ret_kernel(lg_ref, gl_ref, q_ref, k_ref, v_ref, gate_ref, cos_ref, sin_ref, gn_ref, *rest,
                L, hd, has_s0, head_major):
    if has_s0:
        s0_ref, y_ref, sout_ref, s_scr = rest
    else:
        y_ref, sout_ref, s_scr = rest
    h = pl.program_id(1)
    c = pl.program_id(2)
    lg = lg_ref[h]

    @pl.when(c == 0)
    def _():
        if has_s0:
            s_scr[...] = s0_ref[0, 0]
        else:
            s_scr[...] = jnp.zeros_like(s_scr)

    half = hd // 2
    cos = cos_ref[...]
    sin = sin_ref[...]

    def rope(x):
        x1, x2 = x[:, :half], x[:, half:]
        return jnp.concatenate([x1 * cos - x2 * sin, x1 * sin + x2 * cos], axis=-1)

    def tile(ref):
        return ref[0, 0] if head_major else ref[...]

    q = rope(tile(q_ref))
    k = rope(tile(k_ref)) * (hd ** -0.5)
    vb = tile(v_ref).astype(BF16)
    n_col = lax.broadcasted_iota(jnp.int32, (L, 1), 0).astype(F32)
    diff = (lax.broadcasted_iota(jnp.int32, (L, L), 0)
            - lax.broadcasted_iota(jnp.int32, (L, L), 1)).astype(F32)
    d_in = jnp.where(diff >= 0, jnp.exp(diff * lg), 0.0)
    qb = q.astype(BF16)
    s = lax.dot_general(qb, k.astype(BF16), (((1,), (1,)), ((), ())),
                        preferred_element_type=F32) * d_in
    inner = jnp.dot(s.astype(BF16), vb, preferred_element_type=F32)
    s_old = s_scr[...]
    cross = jnp.dot(qb, s_old.astype(BF16), preferred_element_type=F32) * jnp.exp((n_col + 1.0) * lg)
    kd = (k * jnp.exp((L - 1.0 - n_col) * lg)).astype(BF16)
    s_new = gl_ref[h] * s_old + lax.dot_general(kd, vb, (((0,), (0,)), ((), ())),
                                                preferred_element_type=F32)
    s_scr[...] = s_new
    o = inner + cross
    xc = o - jnp.mean(o, axis=-1, keepdims=True)
    on = xc * lax.rsqrt(jnp.mean(xc * xc, axis=-1, keepdims=True) + EPS) * gn_ref[...]
    y_ref[...] = (jax.nn.silu(tile(gate_ref)) * on).astype(y_ref.dtype)

    @pl.when(c == pl.num_programs(2) - 1)
    def _():
        sout_ref[0, 0] = s_new


def retention(proj, cos, sin, gn_g, s0, *, B, T, H, hd):
    L = _pick(T, 256)
    nc = T // L
    lg = jnp.log1p(-jnp.exp2(-5.0 - jnp.arange(H, dtype=F32)))
    gl = jnp.exp(L * lg)
    has_s0 = s0 is not None
    head_major = proj.ndim == 4

    def col(sec):
        if head_major:
            return pl.BlockSpec((1, 1, L, hd), lambda b, h, c: (b, sec * H + h, c, 0))
        return pl.BlockSpec((L, hd), lambda b, h, c: (b * nc + c, sec * H + h))

    smem = pl.BlockSpec(memory_space=pltpu.SMEM)
    in_specs = [smem, smem, col(0), col(1), col(2), col(3),
                pl.BlockSpec((L, hd // 2), lambda b, h, c: (c, 0)),
                pl.BlockSpec((L, hd // 2), lambda b, h, c: (c, 0)),
                pl.BlockSpec((1, hd), lambda b, h, c: (0, h))]
    args = [lg, gl, proj, proj, proj, proj, cos, sin, gn_g.reshape(1, H * hd)]
    state_spec = pl.BlockSpec((1, 1, hd, hd), lambda b, h, c: (b, h, 0, 0))
    if has_s0:
        in_specs.append(state_spec)
        args.append(s0)
    return pl.pallas_call(
        functools.partial(_ret_kernel, L=L, hd=hd, has_s0=has_s0, head_major=head_major),
        out_shape=(jax.ShapeDtypeStruct((B * T, H * hd), BF16),
                   jax.ShapeDtypeStruct((B, H, hd, hd), F32)),
        grid=(B, H, nc),
        in_specs=in_specs,
        out_specs=(pl.BlockSpec((L, hd), lambda b, h, c: (b * nc + c, h)), state_spec),
        scratch_shapes=[pltpu.VMEM((hd, hd), F32)],
        compiler_params=_cparams(("parallel", "parallel", "arbitrary")),
        name="retention",
    )(*args)


def _cmlp_kernel(u_ref, vb_ref, lng_ref, lnb_ref, ws_ref, bst_ref, y_ref, *zv_out, G, gd, L):
    zu = jax.nn.gelu(u_ref[...])
    gv = jax.nn.gelu(vb_ref[...])
    xc = gv - jnp.mean(gv, axis=-1, keepdims=True)
    zv = xc * lax.rsqrt(jnp.mean(xc * xc, axis=-1, keepdims=True) + EPS) * lng_ref[...] + lnb_ref[...]
    if zv_out:
        zv_out[0][...] = zv
    keep = (lax.broadcasted_iota(jnp.int32, (L, L), 0) >= lax.broadcasted_iota(jnp.int32, (L, L), 1))
    for g in range(G):
        w = jnp.where(keep, ws_ref[g, :L, :L], 0.0).astype(BF16)
        mixed = jnp.dot(w, zv[:, g * gd:(g + 1) * gd].astype(BF16), preferred_element_type=F32)
        mixed = mixed + bst_ref[:, g:g + 1]
        y_ref[:, g * gd:(g + 1) * gd] = (zu[:, g * gd:(g + 1) * gd] * mixed).astype(y_ref.dtype)


def cmlp(proj, ln_g, ln_b, ws, bs, *, B, T, u_col, want_zv):
    G, chunk, _ = ws.shape
    W = ln_g.shape[0]
    gd = W // G
    L = min(T, chunk)
    nc = T // L
    out_shape = [jax.ShapeDtypeStruct((B * T, W), BF16)]
    out_specs = [pl.BlockSpec((L, W), lambda b, c: (b * nc + c, 0))]
    if want_zv:
        out_shape.append(jax.ShapeDtypeStruct((B * T, W), F32))
        out_specs.append(pl.BlockSpec((L, W), lambda b, c: (b * nc + c, 0)))
    res = pl.pallas_call(
        functools.partial(_cmlp_kernel, G=G, gd=gd, L=L),
        out_shape=tuple(out_shape),
        grid=(B, nc),
        in_specs=[pl.BlockSpec((L, W), lambda b, c: (b * nc + c, u_col)),
                  pl.BlockSpec((L, W), lambda b, c: (b * nc + c, u_col + 1)),
                  pl.BlockSpec((1, W), lambda b, c: (0, 0)),
                  pl.BlockSpec((1, W), lambda b, c: (0, 0)),
                  pl.BlockSpec((G, chunk, chunk), lambda b, c: (0, 0, 0)),
                  pl.BlockSpec((L, G), lambda b, c: (0, 0))],
        out_specs=tuple(out_specs),
        compiler_params=_cparams(("parallel", "parallel")),
        name="cmlp",
    )(proj, proj, ln_g.reshape(1, W), ln_b.reshape(1, W), ws, bs[:, :L].T)
    return res if want_zv else (res[0], None)


def _mem_kernel(x_ref, g_ref, wq_ref, qn_ref, mk_ref, mv_ref, wo_ref, o_ref, *, heads, hd):
    x = x_ref[...]
    h = _rms(x, g_ref[...]).astype(BF16)
    q = jnp.dot(h, wq_ref[...], preferred_element_type=F32)
    mk = mk_ref[0]
    mv = mv_ref[0]
    outs = []
    for hh in range(heads):
        sl = slice(hh * hd, (hh + 1) * hd)
        qh = _rms(q[:, sl], qn_ref[...]).astype(BF16)
        s = lax.dot_general(qh, mk[:, sl].astype(BF16), (((1,), (1,)), ((), ())),
                            preferred_element_type=F32) * (hd ** -0.5)
        e = jnp.exp(s - jnp.max(s, axis=-1, keepdims=True))
        p = (e / jnp.sum(e, axis=-1, keepdims=True)).astype(BF16)
        outs.append(jnp.dot(p, mv[:, sl].astype(BF16), preferred_element_type=F32))
    o = jnp.concatenate(outs, axis=-1).astype(BF16)
    o_ref[...] = x + jnp.dot(o, wo_ref[...], preferred_element_type=F32)


def mem_attn(x, g, wq, qn_g, mk, mv, wo, *, B, T, heads):
    n, d = x.shape
    mw = wq.shape[1]
    hd = mw // heads
    mlen = mk.shape[1]
    tm = _pick(T, 256)
    tpb = T // tm
    return pl.pallas_call(
        functools.partial(_mem_kernel, heads=heads, hd=hd),
        out_shape=jax.ShapeDtypeStruct((n, d), F32),
        grid=(n // tm,),
        in_specs=[pl.BlockSpec((tm, d), lambda i: (i, 0)),
                  pl.BlockSpec((1, d), lambda i: (0, 0)),
                  pl.BlockSpec((d, mw), lambda i: (0, 0)),
                  pl.BlockSpec((1, hd), lambda i: (0, 0)),
                  pl.BlockSpec((1, mlen, mw), lambda i: (i // tpb, 0, 0)),
                  pl.BlockSpec((1, mlen, mw), lambda i: (i // tpb, 0, 0)),
                  pl.BlockSpec((mw, d), lambda i: (0, 0))],
        out_specs=pl.BlockSpec((tm, d), lambda i: (i, 0)),
        compiler_params=_cparams(("parallel",)),
        name="mem_attn",
    )(x, g.reshape(1, d), wq, qn_g.reshape(1, hd), mk, mv, wo)


def _router_kernel(x_ref, g_ref, w_ref, b_ref, ids_ref, gates_ref, *, G, E):
    h = _rms(x_ref[...], g_ref[...])
    h1 = h.astype(BF16)
    h2 = (h - h1.astype(F32)).astype(BF16)
    a = jnp.dot(h1, w_ref[...], preferred_element_type=F32)
    logits = (a[:, :LANES] + a[:, LANES:] + jnp.dot(h2, w_ref[:, :LANES], preferred_element_type=F32)
              + b_ref[...])
    lane = lax.broadcasted_iota(jnp.int32, logits.shape, 1)
    neg = -jnp.inf
    is_g = lane < G
    lgm = jnp.where(is_g, logits, neg)
    gmax = jnp.max(lgm, axis=-1, keepdims=True)
    gsel = jnp.min(jnp.where(lgm == gmax, lane, LANES), axis=-1, keepdims=True)
    p_grp = 1.0 / jnp.sum(jnp.where(is_g, jnp.exp(logits - gmax), 0.0), axis=-1, keepdims=True)
    lo = G + gsel * E
    lem = jnp.where((lane >= lo) & (lane < lo + E), logits, neg)
    v1 = jnp.max(lem, axis=-1, keepdims=True)
    i1 = jnp.min(jnp.where(lem == v1, lane, LANES), axis=-1, keepdims=True)
    lem2 = jnp.where(lane == i1, neg, lem)
    v2 = jnp.max(lem2, axis=-1, keepdims=True)
    i2 = jnp.min(jnp.where(lem2 == v2, lane, LANES), axis=-1, keepdims=True)
    e2 = jnp.exp(v2 - v1)
    den = 1.0 + e2
    ids_ref[...] = jnp.where(lane == 0, i1 - G, jnp.where(lane == 1, i2 - G, 0))
    gates_ref[...] = jnp.where(lane == 0, p_grp / den, jnp.where(lane == 1, p_grp * e2 / den, 0.0))


def moe_router(x, g, w_rg, b_rg, w_re, b_re):
    n, d = x.shape
    G = w_rg.shape[1]
    E = w_re.shape[1] // G
    assert G + G * E <= LANES
    pad = LANES - G - G * E
    w = jnp.concatenate([w_rg, w_re, jnp.zeros((d, pad), F32)], axis=1)
    w1 = w.astype(BF16)
    w = jnp.concatenate([w1, (w - w1.astype(F32)).astype(BF16)], axis=1)
    b = jnp.concatenate([b_rg, b_re, jnp.zeros((pad,), F32)]).reshape(1, LANES)
    tm = _pick(n, 256)
    row = pl.BlockSpec((tm, LANES), lambda i: (i, 0))
    return pl.pallas_call(
        functools.partial(_router_kernel, G=G, E=E),
        out_shape=(jax.ShapeDtypeStruct((n, LANES), jnp.int32),
                   jax.ShapeDtypeStruct((n, LANES), F32)),
        grid=(n // tm,),
        in_specs=[pl.BlockSpec((tm, d), lambda i: (i, 0)),
                  pl.BlockSpec((1, d), lambda i: (0, 0)),
                  pl.BlockSpec((d, 2 * LANES), lambda i: (0, 0)),
                  pl.BlockSpec((1, LANES), lambda i: (0, 0))],
        out_specs=(row, row),
        compiler_params=_cparams(("parallel",)),
        name="moe_router",
    )(x, g.reshape(1, d), w, b)


def moe_dispatch(ids, n_experts, tm):
    n = ids.shape[0]
    flat = ids[:, :2].reshape(-1)
    onehot = (flat[:, None] == jnp.arange(n_experts, dtype=jnp.int32)[None, :]).astype(jnp.int32)
    csum = jnp.cumsum(onehot, axis=0)
    counts = csum[-1]
    rank = jnp.take_along_axis(csum, flat[:, None], axis=1)[:, 0] - 1
    padded = ((counts + tm - 1) // tm) * tm
    ends = jnp.cumsum(padded)
    starts = ends - padded
    pos = starts[flat] + rank
    max_tiles = (2 * n) // tm + n_experts + 1
    tile_start = jnp.arange(max_tiles, dtype=jnp.int32) * tm
    tile_expert = jnp.minimum(jnp.sum((ends[None, :] <= tile_start[:, None]).astype(jnp.int32), axis=1),
                              n_experts - 1)
    order = jnp.argsort(flat, stable=True).astype(jnp.int32)
    raw_starts = jnp.cumsum(counts) - counts
    row_e = jnp.repeat(tile_expert, tm)
    k_in_e = jnp.arange(max_tiles * tm, dtype=jnp.int32) - starts[row_e]
    src = jnp.minimum(raw_starts[row_e] + k_in_e, 2 * n - 1)
    row_token = jnp.where(k_in_e < counts[row_e], order[src] // 2, 0).astype(jnp.int32)
    n_tiles = (ends[-1] // tm).astype(jnp.int32).reshape(1)
    return row_token, pos.astype(jnp.int32), tile_expert, n_tiles


def _row_copy(src_hbm, dst, src_row, r, sem):
    return pltpu.make_async_copy(src_hbm.at[pl.ds(src_row, 1)], dst.at[pl.ds(r, 1)], sem)


def _row_gather_start(src_hbm, dst, idx_ref, base, rows, sem):
    for r in range(rows):
        _row_copy(src_hbm, dst, idx_ref[base + r], r, sem).start()


def _row_gather_wait(src_hbm, dst, rows, sem):
    for r in range(rows):
        _row_copy(src_hbm, dst, 0, r, sem).wait()


def _ffn_kernel(te_ref, rt_ref, nt_ref, x_hbm, g_ref, wg_ref, wu_ref, wd_ref, y_ref, xbuf, hbuf, sem, *, tm):
    del te_ref
    i = pl.program_id(0)
    nt = nt_ref[0]
    slot = i % 2

    @pl.when(i == 0)
    def _():
        _row_gather_start(x_hbm, xbuf.at[0], rt_ref, 0, tm, sem.at[0])

    @pl.when(i < nt)
    def _():
        _row_gather_wait(x_hbm, xbuf.at[slot], tm, sem.at[slot])
        hbuf[...] = _rms(xbuf[slot], g_ref[...]).astype(BF16)
        _row_gather_start(x_hbm, xbuf.at[1 - slot], rt_ref, (i + 1) * tm, tm, sem.at[1 - slot])
        h = hbuf[...]
        a = (jax.nn.silu(jnp.dot(h, wg_ref[0], preferred_element_type=F32))
             * jnp.dot(h, wu_ref[0], preferred_element_type=F32))
        y_ref[...] = jnp.dot(a.astype(BF16), wd_ref[0], preferred_element_type=F32)

    @pl.when(i == nt)
    def _():
        _row_gather_wait(x_hbm, xbuf.at[slot], tm, sem.at[slot])

    @pl.when(i >= nt)
    def _():
        y_ref[...] = jnp.zeros_like(y_ref)


def moe_ffn(x, g, wg, wu, wd, row_token, tile_expert, n_tiles, *, tm, first_expert):
    n, d = x.shape
    f = wg.shape[2]
    max_tiles = tile_expert.shape[0]
    grid_spec = pltpu.PrefetchScalarGridSpec(
        num_scalar_prefetch=3,
        grid=(max_tiles,),
        in_specs=[pl.BlockSpec(memory_space=pl.ANY),
                  pl.BlockSpec((1, d), lambda i, te, rt, nt: (0, 0)),
                  pl.BlockSpec((1, d, f), lambda i, te, rt, nt: (first_expert + te[i], 0, 0)),
                  pl.BlockSpec((1, d, f), lambda i, te, rt, nt: (first_expert + te[i], 0, 0)),
                  pl.BlockSpec((1, f, d), lambda i, te, rt, nt: (first_expert + te[i], 0, 0))],
        out_specs=pl.BlockSpec((tm, d), lambda i, te, rt, nt: (i, 0)),
        scratch_shapes=[pltpu.VMEM((2, tm, d), F32), pltpu.VMEM((tm, d), BF16),
                        pltpu.SemaphoreType.DMA((2,))],
    )
    return pl.pallas_call(
        functools.partial(_ffn_kernel, tm=tm),
        out_shape=jax.ShapeDtypeStruct((max_tiles * tm, d), F32),
        grid_spec=grid_spec,
        compiler_params=_cparams(("arbitrary",)),
        name="moe_ffn",
    )(tile_expert, row_token, n_tiles, x, g.reshape(1, d), wg, wu, wd)


def _combine_kernel(pos_ref, x_ref, gates_ref, y_hbm, o_ref, ybuf, sem, *, tc):
    i = pl.program_id(0)
    nsteps = pl.num_programs(0)
    slot = i % 2

    @pl.when(i == 0)
    def _():
        _row_gather_start(y_hbm, ybuf.at[0], pos_ref, 0, 2 * tc, sem.at[0])

    @pl.when(i + 1 < nsteps)
    def _():
        _row_gather_start(y_hbm, ybuf.at[1 - slot], pos_ref, (i + 1) * 2 * tc, 2 * tc, sem.at[1 - slot])

    _row_gather_wait(y_hbm, ybuf.at[slot], 2 * tc, sem.at[slot])
    gates = gates_ref[...]
    o_ref[...] = (x_ref[...] + gates[:, 0:1] * ybuf[slot, pl.ds(0, tc)]
                  + gates[:, 1:2] * ybuf[slot, pl.ds(tc, tc)])


def moe_combine(x, gates, y_rows, pos, *, tc):
    n, d = x.shape
    nt = n // tc
    pos_tiled = pos.reshape(nt, tc, 2).transpose(0, 2, 1).reshape(-1)
    grid_spec = pltpu.PrefetchScalarGridSpec(
        num_scalar_prefetch=1,
        grid=(nt,),
        in_specs=[pl.BlockSpec((tc, d), lambda i, p: (i, 0)),
                  pl.BlockSpec((tc, LANES), lambda i, p: (i, 0)),
                  pl.BlockSpec(memory_space=pl.ANY)],
        out_specs=pl.BlockSpec((tc, d), lambda i, p: (i, 0)),
        scratch_shapes=[pltpu.VMEM((2, 2 * tc, d), F32), pltpu.SemaphoreType.DMA((2,))],
    )
    return pl.pallas_call(
        functools.partial(_combine_kernel, tc=tc),
        out_shape=jax.ShapeDtypeStruct((n, d), F32),
        grid_spec=grid_spec,
        compiler_params=_cparams(("arbitrary",)),
        name="moe_combine",
    )(pos_tiled, x, gates, y_rows)


def hier_moe(x, g, w_rg, b_rg, w_re, b_re, wg, wu, wd, *, first_expert, n_experts):
    n = x.shape[0]
    tm = _pick(n, 256)
    ids, gates = moe_router(x, g, w_rg, b_rg, w_re, b_re)
    row_token, pos, tile_expert, n_tiles = moe_dispatch(ids, n_experts, tm)
    y_rows = moe_ffn(x, g, wg, wu, wd, row_token, tile_expert, n_tiles, tm=tm, first_expert=first_expert)
    return moe_combine(x, gates, y_rows, pos, tc=_pick(n, 128))


def _cumsum_kernel(x_ref, c_ref, ct_ref, carry, *, L):
    @pl.when(pl.program_id(1) == 0)
    def _():
        carry[...] = jnp.zeros_like(carry)

    tri = (lax.broadcasted_iota(jnp.int32, (L, L), 0)
           >= lax.broadcasted_iota(jnp.int32, (L, L), 1)).astype(F32)
    c = jnp.dot(tri, x_ref[0], preferred_element_type=F32,
                precision=lax.Precision.HIGHEST) + carry[...]
    c_ref[0] = c * LOG2E
    ct_ref[0] = (c * LOG2E).T
    carry[...] = c[L - 1:L, :]


def cumsum_time(x):
    B, T, w = x.shape
    L = _pick(T, 256)
    return pl.pallas_call(
        functools.partial(_cumsum_kernel, L=L),
        out_shape=(jax.ShapeDtypeStruct((B, T, w), F32), jax.ShapeDtypeStruct((B, w, T), F32)),
        grid=(B, T // L),
        in_specs=[pl.BlockSpec((1, L, w), lambda b, t: (b, t, 0))],
        out_specs=(pl.BlockSpec((1, L, w), lambda b, t: (b, t, 0)),
                   pl.BlockSpec((1, w, L), lambda b, t: (b, 0, t))),
        scratch_shapes=[pltpu.VMEM((1, w), F32)],
        compiler_params=_cparams(("parallel", "arbitrary")),
        name="cumsum_time",
    )(x)


def _row_bcast(r, width):
    if width % LANES == 0:
        return jnp.tile(r, (1, width // LANES))
    return r[:, :1]


def _softmax_stats(t, cq, m_old, l_old):
    m_new = jnp.maximum(m_old, jnp.max(t, axis=-1, keepdims=True) + cq)
    alpha = jnp.exp2(m_old - m_new)
    p = jnp.exp2(t + _row_bcast(cq - m_new, t.shape[1]))
    l_new = alpha * l_old + jnp.sum(p, axis=-1, keepdims=True)
    return m_new, l_new, alpha, p


def _fox_prompt_kernel(q_ref, k_ref, v_ref, cq_ref, ck_ref, o_ref, m_scr, l_scr, acc_scr,
                       *, HG, tq, tk, hd):
    hg = pl.program_id(1)
    qi = pl.program_id(2)
    ki = pl.program_id(3)

    @pl.when(ki == 0)
    def _():
        m_scr[...] = jnp.full_like(m_scr, -jnp.inf)
        l_scr[...] = jnp.zeros_like(l_scr)
        acc_scr[...] = jnp.zeros_like(acc_scr)

    def run(masked):
        cq_all = cq_ref[...]
        lane = lax.broadcasted_iota(jnp.int32, cq_all.shape, 1)
        if masked:
            keep = (lax.broadcasted_iota(jnp.int32, (tq, tk), 0) + qi * tq
                    >= lax.broadcasted_iota(jnp.int32, (tq, tk), 1) + ki * tk)

        def head(hh, carry):
            s = lax.dot_general(q_ref[0, hh], k_ref[0, hh], (((1,), (1,)), ((), ())),
                                preferred_element_type=F32)
            t = s - ck_ref[0, pl.ds(hh, 1), :]
            if masked:
                t = jnp.where(keep, t, -jnp.inf)
            cq = jnp.sum(jnp.where(lane == hg * HG + hh, cq_all, 0.0), axis=-1, keepdims=True)
            m_new, l_new, alpha, p = _softmax_stats(t, cq, m_scr[hh], l_scr[hh])
            m_scr[hh] = m_new
            l_scr[hh] = l_new
            acc_scr[hh] = alpha * acc_scr[hh] + jnp.dot(p.astype(BF16), v_ref[0, hh],
                                                        preferred_element_type=F32)
            return carry

        lax.fori_loop(0, HG, head, 0, unroll=4)

    @pl.when(ki < qi)
    def _():
        run(False)

    @pl.when(ki == qi)
    def _():
        run(True)
        for hh in range(HG):
            o_ref[:, hh * hd:(hh + 1) * hd] = (acc_scr[hh] / l_scr[hh]).astype(o_ref.dtype)


def fox_prompt(q_hm, k_hm, v_hm, c_tok, c_t, *, B, T, H, hd):
    HG = _pick(H, 16)
    tq = tk = _pick(T, 512)
    nq = T // tq
    kv_spec = pl.BlockSpec((1, HG, tk, hd), lambda b, g, qi, ki: (b, g, jnp.minimum(ki, qi), 0))
    return pl.pallas_call(
        functools.partial(_fox_prompt_kernel, HG=HG, tq=tq, tk=tk, hd=hd),
        out_shape=jax.ShapeDtypeStruct((B * T, H * hd), BF16),
        grid=(B, H // HG, nq, nq),
        in_specs=[pl.BlockSpec((1, HG, tq, hd), lambda b, g, qi, ki: (b, g, qi, 0)),
                  kv_spec, kv_spec,
                  pl.BlockSpec((tq, LANES), lambda b, g, qi, ki: (b * nq + qi, 0)),
                  pl.BlockSpec((1, HG, tk), lambda b, g, qi, ki: (b, g, jnp.minimum(ki, qi)))],
        out_specs=pl.BlockSpec((tq, HG * hd), lambda b, g, qi, ki: (b * nq + qi, g)),
        scratch_shapes=[pltpu.VMEM((HG, tq, LANES), F32), pltpu.VMEM((HG, tq, LANES), F32),
                        pltpu.VMEM((HG, tq, hd), F32)],
        compiler_params=_cparams(("parallel", "parallel", "parallel", "arbitrary")),
        name="fox_prompt",
    )(q_hm, k_hm, v_hm, c_tok, c_t)


def _fox_sample_kernel(q_ref, kn_ref, vn_ref, kp_ref, vp_ref, cq_ref, cn_ref, cp_ref, o_ref,
                       cq_scr, m_scr, l_scr, acc_scr, s_scr, p_scr, *, H, hd, Ts, tk):
    j = pl.program_id(1)

    def rows(hh):
        return slice(hh * Ts, (hh + 1) * Ts)

    def cols(hh):
        return slice(hh * hd, (hh + 1) * hd)

    def update(width, key, value, ck, keep):
        for hh in range(H):
            s = lax.dot_general(q_ref[:, cols(hh)], key(hh), (((1,), (1,)), ((), ())),
                                preferred_element_type=F32)
            t = s - ck(hh)
            s_scr[rows(hh), :width] = t if keep is None else jnp.where(keep, t, -jnp.inf)
        m_new, l_new, alpha, p = _softmax_stats(s_scr[:, :width], cq_scr[...], m_scr[...], l_scr[...])
        m_scr[...] = m_new
        l_scr[...] = l_new
        p_scr[:, :width] = p.astype(BF16)
        for hh in range(H):
            acc_scr[rows(hh)] = alpha[rows(hh)] * acc_scr[rows(hh)] + jnp.dot(
                p_scr[rows(hh), :width], value(hh), preferred_element_type=F32)

    @pl.when(j == 0)
    def _():
        m_scr[...] = jnp.full_like(m_scr, -jnp.inf)
        l_scr[...] = jnp.zeros_like(l_scr)
        acc_scr[...] = jnp.zeros_like(acc_scr)
        for hh in range(H):
            cq_scr[rows(hh)] = jnp.broadcast_to(cq_ref[0, :, hh:hh + 1], (Ts, LANES))
        keep = (lax.broadcasted_iota(jnp.int32, (Ts, Ts), 0) >= lax.broadcasted_iota(jnp.int32, (Ts, Ts), 1))
        update(Ts, lambda hh: kn_ref[:, cols(hh)].astype(BF16), lambda hh: vn_ref[:, cols(hh)].astype(BF16),
               lambda hh: cn_ref[0, hh:hh + 1, :Ts], keep)

    @pl.when(j > 0)
    def _():
        update(tk, lambda hh: kp_ref[pl.ds(hh, tk, stride=H), :].astype(BF16),
               lambda hh: vp_ref[pl.ds(hh, tk, stride=H), :].astype(BF16),
               lambda hh: cp_ref[0, hh:hh + 1, :], None)

    @pl.when(j == pl.num_programs(1) - 1)
    def _():
        for hh in range(H):
            o_ref[:, cols(hh)] = (acc_scr[rows(hh)] / l_scr[rows(hh)]).astype(o_ref.dtype)


def fox_sample(q, k_new, v_new, k_past, v_past, c_tok, c_t, *, B, Ts, P, H, hd):
    w = H * hd
    tk = _pick(P, 256)
    npk = P // tk
    assert P % LANES == 0 and Ts <= LANES and P % Ts == 0 and hd == LANES
    past = pl.BlockSpec((tk * H, hd), lambda b, j: (b * npk + jnp.maximum(j - 1, 0), 0))
    new = pl.BlockSpec((Ts, w), lambda b, j: (b, 0))
    return pl.pallas_call(
        functools.partial(_fox_sample_kernel, H=H, hd=hd, Ts=Ts, tk=tk),
        out_shape=jax.ShapeDtypeStruct((B * Ts, w), BF16),
        grid=(B, npk + 1),
        in_specs=[new, new, new, past, past,
                  pl.BlockSpec((1, Ts, LANES), lambda b, j: (b, P // Ts, 0)),
                  pl.BlockSpec((1, LANES, LANES), lambda b, j: (b, 0, P // LANES)),
                  pl.BlockSpec((1, LANES, tk), lambda b, j: (b, 0, jnp.maximum(j - 1, 0)))],
        out_specs=new,
        scratch_shapes=[pltpu.VMEM((H * Ts, LANES), F32), pltpu.VMEM((H * Ts, LANES), F32),
                        pltpu.VMEM((H * Ts, LANES), F32), pltpu.VMEM((H * Ts, hd), F32),
                        pltpu.VMEM((H * Ts, tk), F32), pltpu.VMEM((H * Ts, tk), BF16)],
        compiler_params=_cparams(("parallel", "arbitrary")),
        name="fox_sample",
    )(q, k_new, v_new, k_past, v_past, c_tok, c_t, c_t)


def fox_project(h, wq, wk, wv, wf, b_f, qn_g, kn_g, *, B, T, H, hd, head_major):
    m, _ = h.shape
    w = H * hd
    tm = _pick(T, 1024) if head_major else _pick(m, 1024)
    tn = _pick(w, 512)
    tpb = T // tm if head_major else 1
    tok = pl.BlockSpec((tm, tn), lambda i, j: (i, j))
    hm = pl.BlockSpec((1, tn // hd, tm, hd), lambda i, j: (i // tpb, j, i % tpb, 0))
    gain = pl.BlockSpec((1, hd), lambda i, j: (0, 0))
    tok_f32 = jax.ShapeDtypeStruct((m, w), F32)
    tok_bf16 = jax.ShapeDtypeStruct((m, w), BF16)
    hm_bf16 = jax.ShapeDtypeStruct((B, H, T, hd), BF16)

    def proj(wmat, g, norm, scale, shapes, specs, kinds, name):
        ep = functools.partial(_ep_heads, hd=hd, norm=norm, scale=scale, kinds=kinds)
        extras = (g.reshape(1, hd),) if norm else ()
        especs = (gain,) if norm else ()
        return matmul(h, wmat, epilogue=ep, extras=extras, extra_specs=especs,
                      out_shapes=shapes, out_specs=specs, tm=tm, tn=tn, name=name)

    q_scale = hd ** -0.5 * LOG2E
    if head_major:
        (q,) = proj(wq, qn_g, True, q_scale, (hm_bf16,), (hm,), ("head",), "fox_q")
        k, k_hm = proj(wk, kn_g, True, 1.0, (tok_f32, hm_bf16), (tok, hm), ("tok", "head"), "fox_k")
        v, v_hm = proj(wv, None, False, 1.0, (tok_f32, hm_bf16), (tok, hm), ("tok", "head"), "fox_v")
    else:
        (q,) = proj(wq, qn_g, True, q_scale, (tok_bf16,), (tok,), ("tok",), "fox_q")
        (k,) = proj(wk, kn_g, True, 1.0, (tok_f32,), (tok,), ("tok",), "fox_k")
        (v,) = proj(wv, None, False, 1.0, (tok_f32,), (tok,), ("tok",), "fox_v")
        k_hm = v_hm = None
    tmf = _pick(m, 1024)
    logf = matmul(h, wf, epilogue=_ep_logsig, extras=(b_f,),
                  extra_specs=(pl.BlockSpec((1, LANES), lambda i, j: (0, 0)),),
                  out_shapes=jax.ShapeDtypeStruct((m, LANES), F32),
                  out_specs=pl.BlockSpec((tmf, LANES), lambda i, j: (i, 0)),
                  tm=tmf, tn=LANES, name="fox_logf")
    return q, k, v, logf, k_hm, v_hm


def _stack(parts):
    return parts[0][None] if len(parts) == 1 else jnp.stack(parts)


def _rope_tables(pos, hd):
    half = hd // 2
    inv = ROPE_BASE ** (-jnp.arange(half, dtype=F32) / half)
    ang = pos.astype(F32)[:, None] * inv[None, :]
    return jnp.cos(ang), jnp.sin(ang)


def kernel(x_prompt, x_sample, mem_prompt, state_ret, cache_fox_k, cache_fox_v, cache_fox_logf, cache_mem_k, cache_mem_v, norm_mix, norm_mem, norm_moe, even_w_in, ret_gn_g, cmlp_ln_g, cmlp_ln_b, cmlp_ws, cmlp_bs, even_w_out, odd_w_in, fox_b_f, fox_qn_g, fox_kn_g, odd_w_out, mem_src_g, mem_w_q, mem_w_k, mem_w_v, mem_qn_g, mem_kn_g, mem_w_o, moe_w_rg, moe_b_rg, moe_w_re, moe_b_re, moe_w_gate, moe_w_up, moe_w_down):
    Bp, Tp, D = x_prompt.shape
    Bs, Ts, _ = x_sample.shape
    depth = norm_mix.shape[0]
    P = cache_fox_k.shape[2]
    RH, RHD = state_ret.shape[2], state_ret.shape[3]
    RW = RH * RHD
    CW = cmlp_ln_g.shape[1]
    FH, FHD = cache_fox_k.shape[3], cache_fox_k.shape[4]
    FW = FH * FHD
    MH, MHD = cache_mem_k.shape[3], cache_mem_k.shape[4]
    MLEN = mem_prompt.shape[1]
    NG, NE, _, DF = moe_w_gate.shape[1:]
    assert (4 * RW) % CW == 0 and FHD == LANES and FH <= LANES

    xp = x_prompt.reshape(Bp * Tp, D)
    xs = x_sample.reshape(Bs * Ts, D)
    mem2d = mem_prompt.reshape(Bp * MLEN, D)
    cos_p, sin_p = _rope_tables(jnp.arange(Tp), RHD)
    cos_s, sin_s = _rope_tables(P + jnp.arange(Ts), RHD)

    wg_all = moe_w_gate.reshape(depth * NG * NE, D, DF).astype(BF16)
    wu_all = moe_w_up.reshape(depth * NG * NE, D, DF).astype(BF16)
    wd_all = moe_w_down.reshape(depth * NG * NE, DF, D).astype(BF16)

    ret_S_p, ret_S_s, cmlp_v_s = [], [], []
    fk_p, fv_p, fl_p, fk_s, fv_s, fl_s = [], [], [], [], [], []
    mk_p, mv_p = [], []
    for i in range(depth):
        j = i // 2
        hp = rmsnorm_rows(xp, norm_mix[i])
        hs = rmsnorm_rows(xs, norm_mix[i])
        if i % 2 == 0:
            w_in_ret = even_w_in[j][:, :4 * RW].astype(BF16)
            w_in_mlp = even_w_in[j][:, 4 * RW:].astype(BF16)
            w_out_a = even_w_out[j][:RW].astype(BF16)
            w_out_b = even_w_out[j][RW:].astype(BF16)

            def even(h, x, cos, sin, s0, B, T, want_zv, head_major):
                if head_major:
                    tm, tn = _pick(T, 1024), _pick(4 * RW, 512)
                    tpb = T // tm
                    (proj_ret,) = matmul(
                        h, w_in_ret,
                        epilogue=functools.partial(_ep_heads, hd=RHD, norm=False, scale=1.0, kinds=("head",)),
                        out_shapes=(jax.ShapeDtypeStruct((B, 4 * RH, T, RHD), F32),),
                        out_specs=(pl.BlockSpec((1, tn // RHD, tm, RHD),
                                                lambda a, b: (a // tpb, b, a % tpb, 0)),),
                        tm=tm, tn=tn, name="even_in_ret")
                else:
                    proj_ret = matmul_plain(h, w_in_ret, F32, "even_in_ret")
                proj_mlp = matmul_plain(h, w_in_mlp, F32, "even_in_mlp")
                y_a, S = retention(proj_ret, cos, sin, ret_gn_g[j], s0, B=B, T=T, H=RH, hd=RHD)
                y_b, zv = cmlp(proj_mlp, cmlp_ln_g[j], cmlp_ln_b[j], cmlp_ws[j], cmlp_bs[j],
                               B=B, T=T, u_col=0, want_zv=want_zv)
                return matmul2_residual(y_a, w_out_a, y_b, w_out_b, x, "even_out"), S, zv

            xp, Sp, _ = even(hp, xp, cos_p, sin_p, None, Bp, Tp, False, True)
            xs, Ss, zvs = even(hs, xs, cos_s, sin_s, state_ret[j], Bs, Ts, True, False)
            ret_S_p.append(Sp)
            ret_S_s.append(Ss)
            cmlp_v_s.append(zvs.reshape(Bs, Ts, CW))
        else:
            w_in = odd_w_in[j]
            wq = w_in[:, :FW].astype(BF16)
            wk = w_in[:, FW:2 * FW].astype(BF16)
            wv = w_in[:, 2 * FW:3 * FW].astype(BF16)
            wf = jnp.pad(w_in[:, 3 * FW:], ((0, 0), (0, LANES - FH))).astype(BF16)
            b_f = jnp.pad(fox_b_f[j], (0, LANES - FH)).reshape(1, LANES)
            w_out = odd_w_out[j].astype(BF16)
            q, k, v, logf, k_hm, v_hm = fox_project(hp, wq, wk, wv, wf, b_f, fox_qn_g[j], fox_kn_g[j],
                                                    B=Bp, T=Tp, H=FH, hd=FHD, head_major=True)
            c_tok, c_t = cumsum_time(logf.reshape(Bp, Tp, LANES))
            o = fox_prompt(q, k_hm, v_hm, c_tok.reshape(Bp * Tp, LANES), c_t[:, :FH], B=Bp, T=Tp, H=FH, hd=FHD)
            xp = matmul_plain(o, w_out, F32, "odd_out", residual=xp)
            fk_p.append(k.reshape(Bp, Tp, FH, FHD))
            fv_p.append(v.reshape(Bp, Tp, FH, FHD))
            fl_p.append(logf[:, :FH].reshape(Bp, Tp, FH))
            q, k, v, logf, _, _ = fox_project(hs, wq, wk, wv, wf, b_f, fox_qn_g[j], fox_kn_g[j],
                                              B=Bs, T=Ts, H=FH, hd=FHD, head_major=False)
            t_pad = -(-(P + Ts) // 256) * 256
            seq = jnp.concatenate([jnp.pad(cache_fox_logf[j], ((0, 0), (0, 0), (0, LANES - FH))),
                                   logf.reshape(Bs, Ts, LANES),
                                   jnp.zeros((Bs, t_pad - P - Ts, LANES), F32)], axis=1)
            c_tok, c_t = cumsum_time(seq)
            o = fox_sample(q, k, v, cache_fox_k[j].reshape(Bs * P * FH, FHD),
                           cache_fox_v[j].reshape(Bs * P * FH, FHD),
                           c_tok, c_t, B=Bs, Ts=Ts, P=P, H=FH, hd=FHD)
            xs = matmul_plain(o, w_out, F32, "odd_out", residual=xs)
            fk_s.append(k.reshape(Bs, Ts, FH, FHD))
            fv_s.append(v.reshape(Bs, Ts, FH, FHD))
            fl_s.append(logf[:, :FH].reshape(Bs, Ts, FH))
        m_n = rmsnorm_rows(mem2d, mem_src_g[i])
        mw = MH * MHD
        tmm, tnm = _pick(Bp * MLEN, 1024), _pick(mw, 512)
        tile = pl.BlockSpec((tmm, tnm), lambda a, b: (a, b))
        (mk,) = matmul(m_n, mem_w_k[i].astype(BF16),
                       epilogue=functools.partial(_ep_heads, hd=MHD, norm=True, scale=1.0, kinds=("tok",)),
                       extras=(mem_kn_g[i].reshape(1, MHD),),
                       extra_specs=(pl.BlockSpec((1, MHD), lambda a, b: (0, 0)),),
                       out_shapes=(jax.ShapeDtypeStruct((Bp * MLEN, mw), F32),), out_specs=(tile,),
                       tm=tmm, tn=tnm, name="mem_k")
        mv = matmul_plain(m_n, mem_w_v[i].astype(BF16), F32, "mem_v")
        mk_p.append(mk.reshape(Bp, MLEN, MH, MHD))
        mv_p.append(mv.reshape(Bp, MLEN, MH, MHD))
        wq_m = mem_w_q[i].astype(BF16)
        wo_m = mem_w_o[i].astype(BF16)
        xp = mem_attn(xp, norm_mem[i], wq_m, mem_qn_g[i], mk.reshape(Bp, MLEN, mw), mv.reshape(Bp, MLEN, mw),
                      wo_m, B=Bp, T=Tp, heads=MH)
        xs = mem_attn(xs, norm_mem[i], wq_m, mem_qn_g[i], cache_mem_k[i].reshape(Bs, MLEN, mw),
                      cache_mem_v[i].reshape(Bs, MLEN, mw), wo_m, B=Bs, T=Ts, heads=MH)
        moe = (norm_moe[i], moe_w_rg[i], moe_b_rg[i], moe_w_re[i], moe_b_re[i], wg_all, wu_all, wd_all)
        xp = hier_moe(xp, *moe, first_expert=i * NG * NE, n_experts=NG * NE)
        xs = hier_moe(xs, *moe, first_expert=i * NG * NE, n_experts=NG * NE)

    return (xp.reshape(Bp, Tp, D), xs.reshape(Bs, Ts, D),
            _stack(ret_S_p), _stack(ret_S_s), _stack(cmlp_v_s),
            _stack(fk_p), _stack(fv_p), _stack(fl_p),
            _stack(fk_s), _stack(fv_s), _stack(fl_s),
            _stack(mk_p), _stack(mv_p))
```

```python
import functools

import jax
import jax.numpy as jnp
from jax import lax
from jax.experimental import pallas as pl
from jax.experimental.pallas import tpu as pltpu

F32 = jnp.float32
BF16 = jnp.bfloat16
EPS = 1e-6
ROPE_BASE = 10000.0
LOG2E = 1.4426950408889634
LANES = 128
VMEM_LIMIT_BYTES = 56 * 1024 * 1024


def _cparams(semantics):
    return pltpu.CompilerParams(dimension_semantics=semantics,
                                vmem_limit_bytes=VMEM_LIMIT_BYTES)


def _pick(n, pref):
    t = min(n, pref)
    while n % t:
        t //= 2
    assert t >= 1
    return t


def _rms(x, g):
    return x * lax.rsqrt(jnp.mean(x * x, axis=-1, keepdims=True) + EPS) * g


def _rmsnorm_kernel(x_ref, g_ref, o_ref):
    o_ref[...] = _rms(x_ref[...], g_ref[...]).astype(o_ref.dtype)


def rmsnorm_rows(x, g, out_dtype=BF16):
    n, d = x.shape
    tm = _pick(n, 512)
    return pl.pallas_call(
        _rmsnorm_kernel,
        out_shape=jax.ShapeDtypeStruct((n, d), out_dtype),
        grid=(n // tm,),
        in_specs=[pl.BlockSpec((tm, d), lambda i: (i, 0)),
                  pl.BlockSpec((1, d), lambda i: (0, 0))],
        out_specs=pl.BlockSpec((tm, d), lambda i: (i, 0)),
        compiler_params=_cparams(("parallel",)),
        name="rmsnorm_rows",
    )(x, g.reshape(1, d))


def _mm_kernel(*refs, n_extra, epilogue):
    a_ref, w_ref = refs[0], refs[1]
    extra = refs[2:2 + n_extra]
    outs = refs[2 + n_extra:]
    acc = jnp.dot(a_ref[...], w_ref[...], preferred_element_type=F32)
    epilogue(acc, extra, outs)


def _ep_store(acc, extra, outs):
    outs[0][...] = acc.astype(outs[0].dtype)


def _ep_residual(acc, extra, outs):
    outs[0][...] = extra[0][...] + acc


def _ep_logsig(acc, extra, outs):
    outs[0][...] = jax.nn.log_sigmoid(acc + extra[0][...])


def _ep_heads(acc, extra, outs, *, hd, norm, scale, kinds):
    tn = acc.shape[1]
    for hh in range(tn // hd):
        blk = acc[:, hh * hd:(hh + 1) * hd]
        if norm:
            blk = _rms(blk, extra[0][...])
        if scale != 1.0:
            blk = blk * scale
        for o_ref, kind in zip(outs, kinds):
            if kind == "tok":
                o_ref[:, hh * hd:(hh + 1) * hd] = blk.astype(o_ref.dtype)
            else:
                o_ref[0, hh] = blk.astype(o_ref.dtype)


def matmul(a, w, *, epilogue=_ep_store, extras=(), extra_specs=(), out_shapes, out_specs,
           tm, tn, name):
    m, k = a.shape
    k2, n = w.shape
    assert k == k2 and m % tm == 0 and n % tn == 0
    kern = functools.partial(_mm_kernel, n_extra=len(extras), epilogue=epilogue)
    return pl.pallas_call(
        kern,
        out_shape=out_shapes,
        grid=(m // tm, n // tn),
        in_specs=[pl.BlockSpec((tm, k), lambda i, j: (i, 0)),
                  pl.BlockSpec((k, tn), lambda i, j: (0, j))] + list(extra_specs),
        out_specs=out_specs,
        compiler_params=_cparams(("parallel", "parallel")),
        name=name,
    )(a, w, *extras)


def matmul_plain(a, w, out_dtype, name, residual=None):
    m, _ = a.shape
    n = w.shape[1]
    tm, tn = _pick(m, 1024), _pick(n, 512)
    tile = pl.BlockSpec((tm, tn), lambda i, j: (i, j))
    if residual is None:
        return matmul(a, w, out_shapes=jax.ShapeDtypeStruct((m, n), out_dtype), out_specs=tile,
                      tm=tm, tn=tn, name=name)
    return matmul(a, w, epilogue=_ep_residual, extras=(residual,), extra_specs=(tile,),
                  out_shapes=jax.ShapeDtypeStruct((m, n), F32), out_specs=tile,
                  tm=tm, tn=tn, name=name)


def _mm2_residual_kernel(a1_ref, w1_ref, a2_ref, w2_ref, res_ref, o_ref):
    o_ref[...] = (res_ref[...] + jnp.dot(a1_ref[...], w1_ref[...], preferred_element_type=F32)
                  + jnp.dot(a2_ref[...], w2_ref[...], preferred_element_type=F32))


def matmul2_residual(a1, w1, a2, w2, residual, name):
    m, k1 = a1.shape
    k2 = a2.shape[1]
    n = w1.shape[1]
    tm, tn = _pick(m, 1024), _pick(n, 512)
    tile = pl.BlockSpec((tm, tn), lambda i, j: (i, j))
    return pl.pallas_call(
        _mm2_residual_kernel,
        out_shape=jax.ShapeDtypeStruct((m, n), F32),
        grid=(m // tm, n // tn),
        in_specs=[pl.BlockSpec((tm, k1), lambda i, j: (i, 0)),
                  pl.BlockSpec((k1, tn), lambda i, j: (0, j)),
                  pl.BlockSpec((tm, k2), lambda i, j: (i, 0)),
                  pl.BlockSpec((k2, tn), lambda i, j: (0, j)),
                  tile],
        out_specs=tile,
        compiler_params=_cparams(("parallel", "parallel")),
        name=name,
    )(a1, w1, a2, w2, residual)


def _ret_kernel(lg_ref, gl_ref, q_ref, k_ref, v_ref, gate_ref, cos_ref, sin_ref, gn_ref, *rest,
                L, hd, has_s0, head_major):
    if has_s0:
        s0_ref, y_ref, sout_ref, s_scr = rest
    else:
        y_ref, sout_ref, s_scr = rest
    h = pl.program_id(1)
    c = pl.program_id(2)
    lg = lg_ref[h]

    @pl.when(c == 0)
    def _():
        if has_s0:
            s_scr[...] = s0_ref[0, 0]
        else:
            s_scr[...] = jnp.zeros_like(s_scr)

    half = hd // 2
    cos = cos_ref[...]
    sin = sin_ref[...]

    def rope(x):
        x1, x2 = x[:, :half], x[:, half:]
        return jnp.concatenate([x1 * cos - x2 * sin, x1 * sin + x2 * cos], axis=-1)

    def tile(ref):
        return ref[0, 0] if head_major else ref[...]

    q = rope(tile(q_ref))
    k = rope(tile(k_ref)) * (hd ** -0.5)
    vb = tile(v_ref).astype(BF16)
    n_col = lax.broadcasted_iota(jnp.int32, (L, 1), 0).astype(F32)
    diff = (lax.broadcasted_iota(jnp.int32, (L, L), 0)
            - lax.broadcasted_iota(jnp.int32, (L, L), 1)).astype(F32)
    d_in = jnp.where(diff >= 0, jnp.exp(diff * lg), 0.0)
    qb = q.astype(BF16)
    s = lax.dot_general(qb, k.astype(BF16), (((1,), (1,)), ((), ())),
                        preferred_element_type=F32) * d_in
    inner = jnp.dot(s.astype(BF16), vb, preferred_element_type=F32)
    s_old = s_scr[...]
    cross = jnp.dot(qb, s_old.astype(BF16), preferred_element_type=F32) * jnp.exp((n_col + 1.0) * lg)
    kd = (k * jnp.exp((L - 1.0 - n_col) * lg)).astype(BF16)
    s_new = gl_ref[h] * s_old + lax.dot_general(kd, vb, (((0,), (0,)), ((), ())),
                                                preferred_element_type=F32)
    s_scr[...] = s_new
    o = inner + cross
    xc = o - jnp.mean(o, axis=-1, keepdims=True)
    on = xc * lax.rsqrt(jnp.mean(xc * xc, axis=-1, keepdims=True) + EPS) * gn_ref[...]
    y_ref[...] = (jax.nn.silu(tile(gate_ref)) * on).astype(y_ref.dtype)

    @pl.when(c == pl.num_programs(2) - 1)
    def _():
        sout_ref[0, 0] = s_new


def retention(proj, cos, sin, gn_g, s0, *, B, T, H, hd):
    L = _pick(T, 256)
    nc = T // L
    lg = jnp.log1p(-jnp.exp2(-5.0 - jnp.arange(H, dtype=F32)))
    gl = jnp.exp(L * lg)
    has_s0 = s0 is not None
    head_major = proj.ndim == 4

    def col(sec):
        if head_major:
            return pl.BlockSpec((1, 1, L, hd), lambda b, h, c: (b, sec * H + h, c, 0))
        return pl.BlockSpec((L, hd), lambda b, h, c: (b * nc + c, sec * H + h))

    smem = pl.BlockSpec(memory_space=pltpu.SMEM)
    in_specs = [smem, smem, col(0), col(1), col(2), col(3),
                pl.BlockSpec((L, hd // 2), lambda b, h, c: (c, 0)),
                pl.BlockSpec((L, hd // 2), lambda b, h, c: (c, 0)),
                pl.BlockSpec((1, hd), lambda b, h, c: (0, h))]
    args = [lg, gl, proj, proj, proj, proj, cos, sin, gn_g.reshape(1, H * hd)]
    state_spec = pl.BlockSpec((1, 1, hd, hd), lambda b, h, c: (b, h, 0, 0))
    if has_s0:
        in_specs.append(state_spec)
        args.append(s0)
    return pl.pallas_call(
        functools.partial(_ret_kernel, L=L, hd=hd, has_s0=has_s0, head_major=head_major),
        out_shape=(jax.ShapeDtypeStruct((B * T, H * hd), BF16),
                   jax.ShapeDtypeStruct((B, H, hd, hd), F32)),
        grid=(B, H, nc),
        in_specs=in_specs,
        out_specs=(pl.BlockSpec((L, hd), lambda b, h, c: (b * nc + c, h)), state_spec),
        scratch_shapes=[pltpu.VMEM((hd, hd), F32)],
        compiler_params=_cparams(("parallel", "parallel", "arbitrary")),
        name="retention",
    )(*args)


def _cmlp_kernel(u_ref, vb_ref, lng_ref, lnb_ref, ws_ref, bst_ref, y_ref, *zv_out, G, gd, L):
    zu = jax.nn.gelu(u_ref[...])
    gv = jax.nn.gelu(vb_ref[...])
    xc = gv - jnp.mean(gv, axis=-1, keepdims=True)
    zv = xc * lax.rsqrt(jnp.mean(xc * xc, axis=-1, keepdims=True) + EPS) * lng_ref[...] + lnb_ref[...]
    if zv_out:
        zv_out[0][...] = zv
    keep = (lax.broadcasted_iota(jnp.int32, (L, L), 0) >= lax.broadcasted_iota(jnp.int32, (L, L), 1))
    for g in range(G):
        w = jnp.where(keep, ws_ref[g, :L, :L], 0.0).astype(BF16)
        mixed = jnp.dot(w, zv[:, g * gd:(g + 1) * gd].astype(BF16), preferred_element_type=F32)
        mixed = mixed + bst_ref[:, g:g + 1]
        y_ref[:, g * gd:(g + 1) * gd] = (zu[:, g * gd:(g + 1) * gd] * mixed).astype(y_ref.dtype)


def cmlp(proj, ln_g, ln_b, ws, bs, *, B, T, u_col, want_zv):
    G, chunk, _ = ws.shape
    W = ln_g.shape[0]
    gd = W // G
    L = min(T, chunk)
    nc = T // L
    out_shape = [jax.ShapeDtypeStruct((B * T, W), BF16)]
    out_specs = [pl.BlockSpec((L, W), lambda b, c: (b * nc + c, 0))]
    if want_zv:
        out_shape.append(jax.ShapeDtypeStruct((B * T, W), F32))
        out_specs.append(pl.BlockSpec((L, W), lambda b, c: (b * nc + c, 0)))
    res = pl.pallas_call(
        functools.partial(_cmlp_kernel, G=G, gd=gd, L=L),
        out_shape=tuple(out_shape),
        grid=(B, nc),
        in_specs=[pl.BlockSpec((L, W), lambda b, c: (b * nc + c, u_col)),
                  pl.BlockSpec((L, W), lambda b, c: (b * nc + c, u_col + 1)),
                  pl.BlockSpec((1, W), lambda b, c: (0, 0)),
                  pl.BlockSpec((1, W), lambda b, c: (0, 0)),
                  pl.BlockSpec((G, chunk, chunk), lambda b, c: (0, 0, 0)),
                  pl.BlockSpec((L, G), lambda b, c: (0, 0))],
        out_specs=tuple(out_specs),
        compiler_params=_cparams(("parallel", "parallel")),
        name="cmlp",
    )(proj, proj, ln_g.reshape(1, W), ln_b.reshape(1, W), ws, bs[:, :L].T)
    return res if want_zv else (res[0], None)


def _mem_kernel(x_ref, g_ref, wq_ref, qn_ref, mk_ref, mv_ref, wo_ref, o_ref, *, heads, hd):
    x = x_ref[...]
    h = _rms(x, g_ref[...]).astype(BF16)
    q = jnp.dot(h, wq_ref[...], preferred_element_type=F32)
    mk = mk_ref[0]
    mv = mv_ref[0]
    outs = []
    for hh in range(heads):
        sl = slice(hh * hd, (hh + 1) * hd)
        qh = _rms(q[:, sl], qn_ref[...]).astype(BF16)
        s = lax.dot_general(qh, mk[:, sl].astype(BF16), (((1,), (1,)), ((), ())),
                            preferred_element_type=F32) * (hd ** -0.5)
        e = jnp.exp(s - jnp.max(s, axis=-1, keepdims=True))
        p = (e / jnp.sum(e, axis=-1, keepdims=True)).astype(BF16)
        outs.append(jnp.dot(p, mv[:, sl].astype(BF16), preferred_element_type=F32))
    o = jnp.concatenate(outs, axis=-1).astype(BF16)
    o_ref[...] = x + jnp.dot(o, wo_ref[...], preferred_element_type=F32)


def mem_attn(x, g, wq, qn_g, mk, mv, wo, *, B, T, heads):
    n, d = x.shape
    mw = wq.shape[1]
    hd = mw // heads
    mlen = mk.shape[1]
    tm = _pick(T, 256)
    tpb = T // tm
    return pl.pallas_call(
        functools.partial(_mem_kernel, heads=heads, hd=hd),
        out_shape=jax.ShapeDtypeStruct((n, d), F32),
        grid=(n // tm,),
        in_specs=[pl.BlockSpec((tm, d), lambda i: (i, 0)),
                  pl.BlockSpec((1, d), lambda i: (0, 0)),
                  pl.BlockSpec((d, mw), lambda i: (0, 0)),
                  pl.BlockSpec((1, hd), lambda i: (0, 0)),
                  pl.BlockSpec((1, mlen, mw), lambda i: (i // tpb, 0, 0)),
                  pl.BlockSpec((1, mlen, mw), lambda i: (i // tpb, 0, 0)),
                  pl.BlockSpec((mw, d), lambda i: (0, 0))],
        out_specs=pl.BlockSpec((tm, d), lambda i: (i, 0)),
        compiler_params=_cparams(("parallel",)),
        name="mem_attn",
    )(x, g.reshape(1, d), wq, qn_g.reshape(1, hd), mk, mv, wo)


def _router_kernel(x_ref, g_ref, w_ref, b_ref, ids_ref, gates_ref, *, G, E):
    h = _rms(x_ref[...], g_ref[...])
    h1 = h.astype(BF16)
    h2 = (h - h1.astype(F32)).astype(BF16)
    a = jnp.dot(h1, w_ref[...], preferred_element_type=F32)
    logits = (a[:, :LANES] + a[:, LANES:] + jnp.dot(h2, w_ref[:, :LANES], preferred_element_type=F32)
              + b_ref[...])
    lane = lax.broadcasted_iota(jnp.int32, logits.shape, 1)
    neg = -jnp.inf
    is_g = lane < G
    lgm = jnp.where(is_g, logits, neg)
    gmax = jnp.max(lgm, axis=-1, keepdims=True)
    gsel = jnp.min(jnp.where(lgm == gmax, lane, LANES), axis=-1, keepdims=True)
    p_grp = 1.0 / jnp.sum(jnp.where(is_g, jnp.exp(logits - gmax), 0.0), axis=-1, keepdims=True)
    lo = G + gsel * E
    lem = jnp.where((lane >= lo) & (lane < lo + E), logits, neg)
    v1 = jnp.max(lem, axis=-1, keepdims=True)
    i1 = jnp.min(jnp.where(lem == v1, lane, LANES), axis=-1, keepdims=True)
    lem2 = jnp.where(lane == i1, neg, lem)
    v2 = jnp.max(lem2, axis=-1, keepdims=True)
    i2 = jnp.min(jnp.where(lem2 == v2, lane, LANES), axis=-1, keepdims=True)
    e2 = jnp.exp(v2 - v1)
    den = 1.0 + e2
    ids_ref[...] = jnp.where(lane == 0, i1 - G, jnp.where(lane == 1, i2 - G, 0))
    gates_ref[...] = jnp.where(lane == 0, p_grp / den, jnp.where(lane == 1, p_grp * e2 / den, 0.0))


def moe_router(x, g, w_rg, b_rg, w_re, b_re):
    n, d = x.shape
    G = w_rg.shape[1]
    E = w_re.shape[1] // G
    assert G + G * E <= LANES
    pad = LANES - G - G * E
    w = jnp.concatenate([w_rg, w_re, jnp.zeros((d, pad), F32)], axis=1)
    w1 = w.astype(BF16)
    w = jnp.concatenate([w1, (w - w1.astype(F32)).astype(BF16)], axis=1)
    b = jnp.concatenate([b_rg, b_re, jnp.zeros((pad,), F32)]).reshape(1, LANES)
    tm = _pick(n, 256)
    row = pl.BlockSpec((tm, LANES), lambda i: (i, 0))
    return pl.pallas_call(
        functools.partial(_router_kernel, G=G, E=E),
        out_shape=(jax.ShapeDtypeStruct((n, LANES), jnp.int32),
                   jax.ShapeDtypeStruct((n, LANES), F32)),
        grid=(n // tm,),
        in_specs=[pl.BlockSpec((tm, d), lambda i: (i, 0)),
                  pl.BlockSpec((1, d), lambda i: (0, 0)),
                  pl.BlockSpec((d, 2 * LANES), lambda i: (0, 0)),
                  pl.BlockSpec((1, LANES), lambda i: (0, 0))],
        out_specs=(row, row),
        compiler_params=_cparams(("parallel",)),
        name="moe_router",
    )(x, g.reshape(1, d), w, b)


def moe_dispatch(ids, n_experts, tm):
    n = ids.shape[0]
    flat = ids[:, :2].reshape(-1)
    experts = jnp.arange(n_experts, dtype=jnp.int32)
    hit = flat[:, None] == experts[None, :]
    counts = jnp.sum(hit.astype(jnp.int32), axis=0)
    padded = ((counts + tm - 1) // tm) * tm
    ends = jnp.cumsum(padded)
    starts = ends - padded
    raw_starts = jnp.cumsum(counts) - counts
    order = jnp.argsort(flat, stable=True).astype(jnp.int32)
    sorted_pos = jnp.argsort(order).astype(jnp.int32)
    pos = sorted_pos + jnp.sum(jnp.where(hit, (starts - raw_starts)[None, :], 0), axis=1)
    max_tiles = (2 * n) // tm + n_experts + 1
    tile_start = jnp.arange(max_tiles, dtype=jnp.int32) * tm
    tile_expert = jnp.minimum(jnp.sum((ends[None, :] <= tile_start[:, None]).astype(jnp.int32), axis=1),
                              n_experts - 1)
    tile_src = jnp.minimum(raw_starts[tile_expert] + tile_start - starts[tile_expert], 2 * n)
    order = jnp.concatenate([order, jnp.arange(tm, dtype=jnp.int32) * 2])
    n_tiles = (ends[-1] // tm).astype(jnp.int32).reshape(1)
    return order, pos.astype(jnp.int32), tile_expert, tile_src.astype(jnp.int32), n_tiles


def _row_copy(src_hbm, dst, src_row, r, sem):
    return pltpu.make_async_copy(src_hbm.at[pl.ds(src_row, 1)], dst.at[pl.ds(r, 1)], sem)


def _row_gather_start(src_hbm, dst, idx_ref, base, rows, sem, shift=0):
    for r in range(rows):
        _row_copy(src_hbm, dst, idx_ref[base + r] >> shift, r, sem).start()


def _row_gather_wait(src_hbm, dst, rows, sem):
    for r in range(rows):
        _row_copy(src_hbm, dst, 0, r, sem).wait()


def _ffn_kernel(te_ref, ts_ref, order_ref, nt_ref, x_hbm, g_ref, wg_ref, wu_ref, wd_ref, y_ref,
                xbuf, hbuf, sem, *, tm):
    del te_ref
    i = pl.program_id(0)
    nt = nt_ref[0]
    slot = i % 2

    @pl.when(i == 0)
    def _():
        _row_gather_start(x_hbm, xbuf.at[0], order_ref, ts_ref[0], tm, sem.at[0], shift=1)

    @pl.when(i < nt)
    def _():
        _row_gather_wait(x_hbm, xbuf.at[slot], tm, sem.at[slot])
        hbuf[...] = _rms(xbuf[slot], g_ref[...]).astype(BF16)
        _row_gather_start(x_hbm, xbuf.at[1 - slot], order_ref, ts_ref[i + 1], tm, sem.at[1 - slot], shift=1)
        h = hbuf[...]
        a = (jax.nn.silu(jnp.dot(h, wg_ref[0], preferred_element_type=F32))
             * jnp.dot(h, wu_ref[0], preferred_element_type=F32))
        y_ref[...] = jnp.dot(a.astype(BF16), wd_ref[0], preferred_element_type=F32)

    @pl.when(i == nt)
    def _():
        _row_gather_wait(x_hbm, xbuf.at[slot], tm, sem.at[slot])

    @pl.when(i >= nt)
    def _():
        y_ref[...] = jnp.zeros_like(y_ref)


def moe_ffn(x, g, wg, wu, wd, order, tile_expert, tile_src, n_tiles, *, tm, first_expert):
    n, d = x.shape
    f = wg.shape[2]
    max_tiles = tile_expert.shape[0]
    grid_spec = pltpu.PrefetchScalarGridSpec(
        num_scalar_prefetch=4,
        grid=(max_tiles,),
        in_specs=[pl.BlockSpec(memory_space=pl.ANY),
                  pl.BlockSpec((1, d), lambda i, te, *_: (0, 0)),
                  pl.BlockSpec((1, d, f), lambda i, te, *_: (first_expert + te[i], 0, 0)),
                  pl.BlockSpec((1, d, f), lambda i, te, *_: (first_expert + te[i], 0, 0)),
                  pl.BlockSpec((1, f, d), lambda i, te, *_: (first_expert + te[i], 0, 0))],
        out_specs=pl.BlockSpec((tm, d), lambda i, te, *_: (i, 0)),
        scratch_shapes=[pltpu.VMEM((2, tm, d), F32), pltpu.VMEM((tm, d), BF16),
                        pltpu.SemaphoreType.DMA((2,))],
    )
    return pl.pallas_call(
        functools.partial(_ffn_kernel, tm=tm),
        out_shape=jax.ShapeDtypeStruct((max_tiles * tm, d), F32),
        grid_spec=grid_spec,
        compiler_params=_cparams(("arbitrary",)),
        name="moe_ffn",
    )(tile_expert, tile_src, order, n_tiles, x, g.reshape(1, d), wg, wu, wd)


def _combine_kernel(pos_ref, x_ref, gates_ref, y_hbm, o_ref, ybuf, sem, *, tc):
    i = pl.program_id(0)
    nsteps = pl.num_programs(0)
    slot = i % 2

    @pl.when(i == 0)
    def _():
        _row_gather_start(y_hbm, ybuf.at[0], pos_ref, 0, 2 * tc, sem.at[0])

    @pl.when(i + 1 < nsteps)
    def _():
        _row_gather_start(y_hbm, ybuf.at[1 - slot], pos_ref, (i + 1) * 2 * tc, 2 * tc, sem.at[1 - slot])

    _row_gather_wait(y_hbm, ybuf.at[slot], 2 * tc, sem.at[slot])
    gates = gates_ref[...]
    o_ref[...] = (x_ref[...] + gates[:, 0:1] * ybuf[slot, pl.ds(0, tc)]
                  + gates[:, 1:2] * ybuf[slot, pl.ds(tc, tc)])


def moe_combine(x, gates, y_rows, pos, *, tc):
    n, d = x.shape
    nt = n // tc
    pos_tiled = pos.reshape(nt, tc, 2).transpose(0, 2, 1).reshape(-1)
    grid_spec = pltpu.PrefetchScalarGridSpec(
        num_scalar_prefetch=1,
        grid=(nt,),
        in_specs=[pl.BlockSpec((tc, d), lambda i, p: (i, 0)),
                  pl.BlockSpec((tc, LANES), lambda i, p: (i, 0)),
                  pl.BlockSpec(memory_space=pl.ANY)],
        out_specs=pl.BlockSpec((tc, d), lambda i, p: (i, 0)),
        scratch_shapes=[pltpu.VMEM((2, 2 * tc, d), F32), pltpu.SemaphoreType.DMA((2,))],
    )
    return pl.pallas_call(
        functools.partial(_combine_kernel, tc=tc),
        out_shape=jax.ShapeDtypeStruct((n, d), F32),
        grid_spec=grid_spec,
        compiler_params=_cparams(("arbitrary",)),
        name="moe_combine",
    )(pos_tiled, x, gates, y_rows)


def hier_moe(x, g, w_rg, b_rg, w_re, b_re, wg, wu, wd, *, first_expert, n_experts):
    n = x.shape[0]
    mean_load = max(1, 2 * n // n_experts)
    tm = min(256, max(32, 1 << (mean_load - 1).bit_length()))
    assert tm <= n
    ids, gates = moe_router(x, g, w_rg, b_rg, w_re, b_re)
    order, pos, tile_expert, tile_src, n_tiles = moe_dispatch(ids, n_experts, tm)
    y_rows = moe_ffn(x, g, wg, wu, wd, order, tile_expert, tile_src, n_tiles, tm=tm,
                     first_expert=first_expert)
    return moe_combine(x, gates, y_rows, pos, tc=_pick(n, 128))


def _cumsum_kernel(x_ref, c_ref, ct_ref, carry, *, L):
    @pl.when(pl.program_id(1) == 0)
    def _():
        carry[...] = jnp.zeros_like(carry)

    tri = (lax.broadcasted_iota(jnp.int32, (L, L), 0)
           >= lax.broadcasted_iota(jnp.int32, (L, L), 1)).astype(F32)
    c = jnp.dot(tri, x_ref[0], preferred_element_type=F32,
                precision=lax.Precision.HIGHEST) + carry[...]
    c_ref[0] = c * LOG2E
    ct_ref[0] = (c * LOG2E).T
    carry[...] = c[L - 1:L, :]


def cumsum_time(x):
    B, T, w = x.shape
    L = _pick(T, 256)
    return pl.pallas_call(
        functools.partial(_cumsum_kernel, L=L),
        out_shape=(jax.ShapeDtypeStruct((B, T, w), F32), jax.ShapeDtypeStruct((B, w, T), F32)),
        grid=(B, T // L),
        in_specs=[pl.BlockSpec((1, L, w), lambda b, t: (b, t, 0))],
        out_specs=(pl.BlockSpec((1, L, w), lambda b, t: (b, t, 0)),
                   pl.BlockSpec((1, w, L), lambda b, t: (b, 0, t))),
        scratch_shapes=[pltpu.VMEM((1, w), F32)],
        compiler_params=_cparams(("parallel", "arbitrary")),
        name="cumsum_time",
    )(x)


def _row_bcast(r, width):
    if width % LANES == 0:
        return jnp.tile(r, (1, width // LANES))
    return r[:, :1]


def _softmax_stats(t, cq, m_old, l_old):
    m_new = jnp.maximum(m_old, jnp.max(t, axis=-1, keepdims=True) + cq)
    alpha = jnp.exp2(m_old - m_new)
    p = jnp.exp2(t + _row_bcast(cq - m_new, t.shape[1]))
    l_new = alpha * l_old + jnp.sum(p, axis=-1, keepdims=True)
    return m_new, l_new, alpha, p


def _fox_prompt_kernel(q_ref, k_ref, v_ref, cq_ref, ck_ref, o_ref, m_scr, l_scr, acc_scr,
                       *, HG, tq, tk, hd):
    hg = pl.program_id(1)
    qi = pl.program_id(2)
    ki = pl.program_id(3)

    @pl.when(ki == 0)
    def _():
        m_scr[...] = jnp.full_like(m_scr, -jnp.inf)
        l_scr[...] = jnp.zeros_like(l_scr)
        acc_scr[...] = jnp.zeros_like(acc_scr)

    def run(masked):
        cq_all = cq_ref[...]
        lane = lax.broadcasted_iota(jnp.int32, cq_all.shape, 1)
        if masked:
            keep = (lax.broadcasted_iota(jnp.int32, (tq, tk), 0) + qi * tq
                    >= lax.broadcasted_iota(jnp.int32, (tq, tk), 1) + ki * tk)

        def head(hh, carry):
            s = lax.dot_general(q_ref[0, hh], k_ref[0, hh], (((1,), (1,)), ((), ())),
                                preferred_element_type=F32)
            t = s - ck_ref[0, pl.ds(hh, 1), :]
            if masked:
                t = jnp.where(keep, t, -jnp.inf)
            cq = jnp.sum(jnp.where(lane == hg * HG + hh, cq_all, 0.0), axis=-1, keepdims=True)
            m_new, l_new, alpha, p = _softmax_stats(t, cq, m_scr[hh], l_scr[hh])
            m_scr[hh] = m_new
            l_scr[hh] = l_new
            acc_scr[hh] = alpha * acc_scr[hh] + jnp.dot(p.astype(BF16), v_ref[0, hh],
                                                        preferred_element_type=F32)
            return carry

        lax.fori_loop(0, HG, head, 0, unroll=4)

    @pl.when(ki < qi)
    def _():
        run(False)

    @pl.when(ki == qi)
    def _():
        run(True)
        for hh in range(HG):
            o_ref[:, hh * hd:(hh + 1) * hd] = (acc_scr[hh] / l_scr[hh]).astype(o_ref.dtype)


def fox_prompt(q_hm, k_hm, v_hm, c_tok, c_t, *, B, T, H, hd):
    HG = _pick(H, 16)
    tq = tk = _pick(T, 512)
    nq = T // tq
    kv_spec = pl.BlockSpec((1, HG, tk, hd), lambda b, g, qi, ki: (b, g, jnp.minimum(ki, qi), 0))
    return pl.pallas_call(
        functools.partial(_fox_prompt_kernel, HG=HG, tq=tq, tk=tk, hd=hd),
        out_shape=jax.ShapeDtypeStruct((B * T, H * hd), BF16),
        grid=(B, H // HG, nq, nq),
        in_specs=[pl.BlockSpec((1, HG, tq, hd), lambda b, g, qi, ki: (b, g, qi, 0)),
                  kv_spec, kv_spec,
                  pl.BlockSpec((tq, LANES), lambda b, g, qi, ki: (b * nq + qi, 0)),
                  pl.BlockSpec((1, HG, tk), lambda b, g, qi, ki: (b, g, jnp.minimum(ki, qi)))],
        out_specs=pl.BlockSpec((tq, HG * hd), lambda b, g, qi, ki: (b * nq + qi, g)),
        scratch_shapes=[pltpu.VMEM((HG, tq, LANES), F32), pltpu.VMEM((HG, tq, LANES), F32),
                        pltpu.VMEM((HG, tq, hd), F32)],
        compiler_params=_cparams(("parallel", "parallel", "parallel", "arbitrary")),
        name="fox_prompt",
    )(q_hm, k_hm, v_hm, c_tok, c_t)


def _fox_sample_kernel(q_ref, kn_ref, vn_ref, kp_ref, vp_ref, cq_ref, cn_ref, cp_ref, o_ref,
                       cq_scr, m_scr, l_scr, acc_scr, s_scr, p_scr, *, H, hd, Ts, tk):
    j = pl.program_id(1)

    def rows(hh):
        return slice(hh * Ts, (hh + 1) * Ts)

    def cols(hh):
        return slice(hh * hd, (hh + 1) * hd)

    def update(width, key, value, ck, keep):
        for hh in range(H):
            s = lax.dot_general(q_ref[:, cols(hh)], key(hh), (((1,), (1,)), ((), ())),
                                preferred_element_type=F32)
            t = s - ck(hh)
            s_scr[rows(hh), :width] = t if keep is None else jnp.where(keep, t, -jnp.inf)
        m_new, l_new, alpha, p = _softmax_stats(s_scr[:, :width], cq_scr[...], m_scr[...], l_scr[...])
        m_scr[...] = m_new
        l_scr[...] = l_new
        p_scr[:, :width] = p.astype(BF16)
        for hh in range(H):
            acc_scr[rows(hh)] = alpha[rows(hh)] * acc_scr[rows(hh)] + jnp.dot(
                p_scr[rows(hh), :width], value(hh), preferred_element_type=F32)

    @pl.when(j == 0)
    def _():
        m_scr[...] = jnp.full_like(m_scr, -jnp.inf)
        l_scr[...] = jnp.zeros_like(l_scr)
        acc_scr[...] = jnp.zeros_like(acc_scr)
        for hh in range(H):
            cq_scr[rows(hh)] = jnp.broadcast_to(cq_ref[0, :, hh:hh + 1], (Ts, LANES))
        keep = (lax.broadcasted_iota(jnp.int32, (Ts, Ts), 0) >= lax.broadcasted_iota(jnp.int32, (Ts, Ts), 1))
        update(Ts, lambda hh: kn_ref[:, cols(hh)].astype(BF16), lambda hh: vn_ref[:, cols(hh)].astype(BF16),
               lambda hh: cn_ref[0, hh:hh + 1, :Ts], keep)

    @pl.when(j > 0)
    def _():
        update(tk, lambda hh: kp_ref[pl.ds(hh, tk, stride=H), :].astype(BF16),
               lambda hh: vp_ref[pl.ds(hh, tk, stride=H), :].astype(BF16),
               lambda hh: cp_ref[0, hh:hh + 1, :], None)

    @pl.when(j == pl.num_programs(1) - 1)
    def _():
        for hh in range(H):
            o_ref[:, cols(hh)] = (acc_scr[rows(hh)] / l_scr[rows(hh)]).astype(o_ref.dtype)


def fox_sample(q, k_new, v_new, k_past, v_past, c_tok, c_t, *, B, Ts, P, H, hd):
    w = H * hd
    tk = _pick(P, 256)
    npk = P // tk
    assert P % LANES == 0 and Ts <= LANES and P % Ts == 0 and hd == LANES
    past = pl.BlockSpec((tk * H, hd), lambda b, j: (b * npk + jnp.maximum(j - 1, 0), 0))
    new = pl.BlockSpec((Ts, w), lambda b, j: (b, 0))
    return pl.pallas_call(
        functools.partial(_fox_sample_kernel, H=H, hd=hd, Ts=Ts, tk=tk),
        out_shape=jax.ShapeDtypeStruct((B * Ts, w), BF16),
        grid=(B, npk + 1),
        in_specs=[new, new, new, past, past,
                  pl.BlockSpec((1, Ts, LANES), lambda b, j: (b, P // Ts, 0)),
                  pl.BlockSpec((1, LANES, LANES), lambda b, j: (b, 0, P // LANES)),
                  pl.BlockSpec((1, LANES, tk), lambda b, j: (b, 0, jnp.maximum(j - 1, 0)))],
        out_specs=new,
        scratch_shapes=[pltpu.VMEM((H * Ts, LANES), F32), pltpu.VMEM((H * Ts, LANES), F32),
                        pltpu.VMEM((H * Ts, LANES), F32), pltpu.VMEM((H * Ts, hd), F32),
                        pltpu.VMEM((H * Ts, tk), F32), pltpu.VMEM((H * Ts, tk), BF16)],
        compiler_params=_cparams(("parallel", "arbitrary")),
        name="fox_sample",
    )(q, k_new, v_new, k_past, v_past, c_tok, c_t, c_t)


def fox_project(h, wq, wk, wv, wf, b_f, qn_g, kn_g, *, B, T, H, hd, head_major):
    m, _ = h.shape
    w = H * hd
    tm = _pick(T, 1024) if head_major else _pick(m, 1024)
    tn = _pick(w, 512)
    tpb = T // tm if head_major else 1
    tok = pl.BlockSpec((tm, tn), lambda i, j: (i, j))
    hm = pl.BlockSpec((1, tn // hd, tm, hd), lambda i, j: (i // tpb, j, i % tpb, 0))
    gain = pl.BlockSpec((1, hd), lambda i, j: (0, 0))
    tok_f32 = jax.ShapeDtypeStruct((m, w), F32)
    tok_bf16 = jax.ShapeDtypeStruct((m, w), BF16)
    hm_bf16 = jax.ShapeDtypeStruct((B, H, T, hd), BF16)

    def proj(wmat, g, norm, scale, shapes, specs, kinds, name):
        ep = functools.partial(_ep_heads, hd=hd, norm=norm, scale=scale, kinds=kinds)
        extras = (g.reshape(1, hd),) if norm else ()
        especs = (gain,) if norm else ()
        return matmul(h, wmat, epilogue=ep, extras=extras, extra_specs=especs,
                      out_shapes=shapes, out_specs=specs, tm=tm, tn=tn, name=name)

    q_scale = hd ** -0.5 * LOG2E
    if head_major:
        (q,) = proj(wq, qn_g, True, q_scale, (hm_bf16,), (hm,), ("head",), "fox_q")
        k, k_hm = proj(wk, kn_g, True, 1.0, (tok_f32, hm_bf16), (tok, hm), ("tok", "head"), "fox_k")
        v, v_hm = proj(wv, None, False, 1.0, (tok_f32, hm_bf16), (tok, hm), ("tok", "head"), "fox_v")
    else:
        (q,) = proj(wq, qn_g, True, q_scale, (tok_bf16,), (tok,), ("tok",), "fox_q")
        (k,) = proj(wk, kn_g, True, 1.0, (tok_f32,), (tok,), ("tok",), "fox_k")
        (v,) = proj(wv, None, False, 1.0, (tok_f32,), (tok,), ("tok",), "fox_v")
        k_hm = v_hm = None
    tmf = _pick(m, 1024)
    logf = matmul(h, wf, epilogue=_ep_logsig, extras=(b_f,),
                  extra_specs=(pl.BlockSpec((1, LANES), lambda i, j: (0, 0)),),
                  out_shapes=jax.ShapeDtypeStruct((m, LANES), F32),
                  out_specs=pl.BlockSpec((tmf, LANES), lambda i, j: (i, 0)),
                  tm=tmf, tn=LANES, name="fox_logf")
    return q, k, v, logf, k_hm, v_hm


def _stack(parts):
    return parts[0][None] if len(parts) == 1 else jnp.stack(parts)


def _rope_tables(pos, hd):
    half = hd // 2
    inv = ROPE_BASE ** (-jnp.arange(half, dtype=F32) / half)
    ang = pos.astype(F32)[:, None] * inv[None, :]
    return jnp.cos(ang), jnp.sin(ang)


def kernel(x_prompt, x_sample, mem_prompt, state_ret, cache_fox_k, cache_fox_v, cache_fox_logf, cache_mem_k, cache_mem_v, norm_mix, norm_mem, norm_moe, even_w_in, ret_gn_g, cmlp_ln_g, cmlp_ln_b, cmlp_ws, cmlp_bs, even_w_out, odd_w_in, fox_b_f, fox_qn_g, fox_kn_g, odd_w_out, mem_src_g, mem_w_q, mem_w_k, mem_w_v, mem_qn_g, mem_kn_g, mem_w_o, moe_w_rg, moe_b_rg, moe_w_re, moe_b_re, moe_w_gate, moe_w_up, moe_w_down):
    Bp, Tp, D = x_prompt.shape
    Bs, Ts, _ = x_sample.shape
    depth = norm_mix.shape[0]
    P = cache_fox_k.shape[2]
    RH, RHD = state_ret.shape[2], state_ret.shape[3]
    RW = RH * RHD
    CW = cmlp_ln_g.shape[1]
    FH, FHD = cache_fox_k.shape[3], cache_fox_k.shape[4]
    FW = FH * FHD
    MH, MHD = cache_mem_k.shape[3], cache_mem_k.shape[4]
    MLEN = mem_prompt.shape[1]
    NG, NE, _, DF = moe_w_gate.shape[1:]
    assert (4 * RW) % CW == 0 and FHD == LANES and FH <= LANES

    xp = x_prompt.reshape(Bp * Tp, D)
    xs = x_sample.reshape(Bs * Ts, D)
    mem2d = mem_prompt.reshape(Bp * MLEN, D)
    cos_p, sin_p = _rope_tables(jnp.arange(Tp), RHD)
    cos_s, sin_s = _rope_tables(P + jnp.arange(Ts), RHD)

    wg_all = moe_w_gate.reshape(depth * NG * NE, D, DF).astype(BF16)
    wu_all = moe_w_up.reshape(depth * NG * NE, D, DF).astype(BF16)
    wd_all = moe_w_down.reshape(depth * NG * NE, DF, D).astype(BF16)

    ret_S_p, ret_S_s, cmlp_v_s = [], [], []
    fk_p, fv_p, fl_p, fk_s, fv_s, fl_s = [], [], [], [], [], []
    mk_p, mv_p = [], []
    for i in range(depth):
        j = i // 2
        hp = rmsnorm_rows(xp, norm_mix[i])
        hs = rmsnorm_rows(xs, norm_mix[i])
        if i % 2 == 0:
            w_in_ret = even_w_in[j][:, :4 * RW].astype(BF16)
            w_in_mlp = even_w_in[j][:, 4 * RW:].astype(BF16)
            w_out_a = even_w_out[j][:RW].astype(BF16)
            w_out_b = even_w_out[j][RW:].astype(BF16)

            def even(h, x, cos, sin, s0, B, T, want_zv, head_major):
                if head_major:
                    tm, tn = _pick(T, 1024), _pick(4 * RW, 512)
                    tpb = T // tm
                    (proj_ret,) = matmul(
                        h, w_in_ret,
                        epilogue=functools.partial(_ep_heads, hd=RHD, norm=False, scale=1.0, kinds=("head",)),
                        out_shapes=(jax.ShapeDtypeStruct((B, 4 * RH, T, RHD), F32),),
                        out_specs=(pl.BlockSpec((1, tn // RHD, tm, RHD),
                                                lambda a, b: (a // tpb, b, a % tpb, 0)),),
                        tm=tm, tn=tn, name="even_in_ret")
                else:
                    proj_ret = matmul_plain(h, w_in_ret, F32, "even_in_ret")
                proj_mlp = matmul_plain(h, w_in_mlp, F32, "even_in_mlp")
                y_a, S = retention(proj_ret, cos, sin, ret_gn_g[j], s0, B=B, T=T, H=RH, hd=RHD)
                y_b, zv = cmlp(proj_mlp, cmlp_ln_g[j], cmlp_ln_b[j], cmlp_ws[j], cmlp_bs[j],
                               B=B, T=T, u_col=0, want_zv=want_zv)
                return matmul2_residual(y_a, w_out_a, y_b, w_out_b, x, "even_out"), S, zv

            xp, Sp, _ = even(hp, xp, cos_p, sin_p, None, Bp, Tp, False, True)
            xs, Ss, zvs = even(hs, xs, cos_s, sin_s, state_ret[j], Bs, Ts, True, False)
            ret_S_p.append(Sp)
            ret_S_s.append(Ss)
            cmlp_v_s.append(zvs.reshape(Bs, Ts, CW))
        else:
            w_in = odd_w_in[j]
            wq = w_in[:, :FW].astype(BF16)
            wk = w_in[:, FW:2 * FW].astype(BF16)
            wv = w_in[:, 2 * FW:3 * FW].astype(BF16)
            wf = jnp.pad(w_in[:, 3 * FW:], ((0, 0), (0, LANES - FH))).astype(BF16)
            b_f = jnp.pad(fox_b_f[j], (0, LANES - FH)).reshape(1, LANES)
            w_out = odd_w_out[j].astype(BF16)
            q, k, v, logf, k_hm, v_hm = fox_project(hp, wq, wk, wv, wf, b_f, fox_qn_g[j], fox_kn_g[j],
                                                    B=Bp, T=Tp, H=FH, hd=FHD, head_major=True)
            c_tok, c_t = cumsum_time(logf.reshape(Bp, Tp, LANES))
            o = fox_prompt(q, k_hm, v_hm, c_tok.reshape(Bp * Tp, LANES), c_t[:, :FH], B=Bp, T=Tp, H=FH, hd=FHD)
            xp = matmul_plain(o, w_out, F32, "odd_out", residual=xp)
            fk_p.append(k.reshape(Bp, Tp, FH, FHD))
            fv_p.append(v.reshape(Bp, Tp, FH, FHD))
            fl_p.append(logf[:, :FH].reshape(Bp, Tp, FH))
            q, k, v, logf, _, _ = fox_project(hs, wq, wk, wv, wf, b_f, fox_qn_g[j], fox_kn_g[j],
                                              B=Bs, T=Ts, H=FH, hd=FHD, head_major=False)
            t_pad = -(-(P + Ts) // 256) * 256
            seq = jnp.concatenate([jnp.pad(cache_fox_logf[j], ((0, 0), (0, 0), (0, LANES - FH))),
                                   logf.reshape(Bs, Ts, LANES),
                                   jnp.zeros((Bs, t_pad - P - Ts, LANES), F32)], axis=1)
            c_tok, c_t = cumsum_time(seq)
            o = fox_sample(q, k, v, cache_fox_k[j].reshape(Bs * P * FH, FHD),
                           cache_fox_v[j].reshape(Bs * P * FH, FHD),
                           c_tok, c_t, B=Bs, Ts=Ts, P=P, H=FH, hd=FHD)
            xs = matmul_plain(o, w_out, F32, "odd_out", residual=xs)
            fk_s.append(k.reshape(Bs, Ts, FH, FHD))
            fv_s.append(v.reshape(Bs, Ts, FH, FHD))
            fl_s.append(logf[:, :FH].reshape(Bs, Ts, FH))
        m_n = rmsnorm_rows(mem2d, mem_src_g[i])
        mw = MH * MHD
        tmm, tnm = _pick(Bp * MLEN, 1024), _pick(mw, 512)
        tile = pl.BlockSpec((tmm, tnm), lambda a, b: (a, b))
        (mk,) = matmul(m_n, mem_w_k[i].astype(BF16),
                       epilogue=functools.partial(_ep_heads, hd=MHD, norm=True, scale=1.0, kinds=("tok",)),
                       extras=(mem_kn_g[i].reshape(1, MHD),),
                       extra_specs=(pl.BlockSpec((1, MHD), lambda a, b: (0, 0)),),
                       out_shapes=(jax.ShapeDtypeStruct((Bp * MLEN, mw), F32),), out_specs=(tile,),
                       tm=tmm, tn=tnm, name="mem_k")
        mv = matmul_plain(m_n, mem_w_v[i].astype(BF16), F32, "mem_v")
        mk_p.append(mk.reshape(Bp, MLEN, MH, MHD))
        mv_p.append(mv.reshape(Bp, MLEN, MH, MHD))
        wq_m = mem_w_q[i].astype(BF16)
        wo_m = mem_w_o[i].astype(BF16)
        xp = mem_attn(xp, norm_mem[i], wq_m, mem_qn_g[i], mk.reshape(Bp, MLEN, mw), mv.reshape(Bp, MLEN, mw),
                      wo_m, B=Bp, T=Tp, heads=MH)
        xs = mem_attn(xs, norm_mem[i], wq_m, mem_qn_g[i], cache_mem_k[i].reshape(Bs, MLEN, mw),
                      cache_mem_v[i].reshape(Bs, MLEN, mw), wo_m, B=Bs, T=Ts, heads=MH)
        moe = (norm_moe[i], moe_w_rg[i], moe_b_rg[i], moe_w_re[i], moe_b_re[i], wg_all, wu_all, wd_all)
        xp = hier_moe(xp, *moe, first_expert=i * NG * NE, n_experts=NG * NE)
        xs = hier_moe(xs, *moe, first_expert=i * NG * NE, n_experts=NG * NE)

    return (xp.reshape(Bp, Tp, D), xs.reshape(Bs, Ts, D),
            _stack(ret_S_p), _stack(ret_S_s), _stack(cmlp_v_s),
            _stack(fk_p), _stack(fv_p), _stack(fl_p),
            _stack(fk_s), _stack(fv_s), _stack(fl_s),
            _stack(mk_p), _stack(mv_p))
```

```python
import functools

import jax
import jax.numpy as jnp
from jax import lax
from jax.experimental import pallas as pl
from jax.experimental.pallas import tpu as pltpu

F32 = jnp.float32
BF16 = jnp.bfloat16
EPS = 1e-6
ROPE_BASE = 10000.0
LOG2E = 1.4426950408889634
LANES = 128
VMEM_LIMIT_BYTES = 56 * 1024 * 1024
FFN_GATHER_SLOTS = 3


def _cparams(semantics):
    return pltpu.CompilerParams(dimension_semantics=semantics,
                                vmem_limit_bytes=VMEM_LIMIT_BYTES)


def _pick(n, pref):
    t = min(n, pref)
    while n % t:
        t //= 2
    assert t >= 1
    return t


def _rms(x, g):
    return x * lax.rsqrt(jnp.mean(x * x, axis=-1, keepdims=True) + EPS) * g


def _rmsnorm_kernel(x_ref, g_ref, o_ref):
    o_ref[...] = _rms(x_ref[...], g_ref[...]).astype(o_ref.dtype)


def rmsnorm_rows(x, g, out_dtype=BF16):
    n, d = x.shape
    tm = _pick(n, 512)
    return pl.pallas_call(
        _rmsnorm_kernel,
        out_shape=jax.ShapeDtypeStruct((n, d), out_dtype),
        grid=(n // tm,),
        in_specs=[pl.BlockSpec((tm, d), lambda i: (i, 0)),
                  pl.BlockSpec((1, d), lambda i: (0, 0))],
        out_specs=pl.BlockSpec((tm, d), lambda i: (i, 0)),
        compiler_params=_cparams(("parallel",)),
        name="rmsnorm_rows",
    )(x, g.reshape(1, d))


def _mm_kernel(*refs, n_extra, epilogue):
    a_ref, w_ref = refs[0], refs[1]
    extra = refs[2:2 + n_extra]
    outs = refs[2 + n_extra:]
    acc = jnp.dot(a_ref[...], w_ref[...].astype(BF16), preferred_element_type=F32)
    epilogue(acc, extra, outs)


def _ep_store(acc, extra, outs):
    outs[0][...] = acc.astype(outs[0].dtype)


def _ep_residual(acc, extra, outs):
    outs[0][...] = extra[0][...] + acc


def _ep_logsig(acc, extra, outs):
    outs[0][...] = jax.nn.log_sigmoid(acc + extra[0][...])


def _ep_heads(acc, extra, outs, *, hd, norm, scale, kinds):
    tn = acc.shape[1]
    for hh in range(tn // hd):
        blk = acc[:, hh * hd:(hh + 1) * hd]
        if norm:
            blk = _rms(blk, extra[0][...])
        if scale != 1.0:
            blk = blk * scale
        for o_ref, kind in zip(outs, kinds):
            if kind == "tok":
                o_ref[:, hh * hd:(hh + 1) * hd] = blk.astype(o_ref.dtype)
            else:
                o_ref[0, hh] = blk.astype(o_ref.dtype)


def matmul(a, w, *, layer, col0=0, n, epilogue=_ep_store, extras=(), extra_specs=(), out_shapes,
           out_specs, tm, tn, name):
    m, k = a.shape
    assert w.shape[1] == k and m % tm == 0 and n % tn == 0 and col0 % tn == 0
    kern = functools.partial(_mm_kernel, n_extra=len(extras), epilogue=epilogue)
    return pl.pallas_call(
        kern,
        out_shape=out_shapes,
        grid=(m // tm, n // tn),
        in_specs=[pl.BlockSpec((tm, k), lambda i, j: (i, 0)),
                  pl.BlockSpec((None, k, tn), lambda i, j: (layer, 0, j + col0 // tn))] + list(extra_specs),
        out_specs=out_specs,
        compiler_params=_cparams(("parallel", "parallel")),
        name=name,
    )(a, w, *extras)


def matmul_plain(a, w, out_dtype, name, *, layer, col0=0, n=None, residual=None):
    m, _ = a.shape
    n = w.shape[2] - col0 if n is None else n
    tm, tn = _pick(m, 1024), _pick(n, 512)
    tile = pl.BlockSpec((tm, tn), lambda i, j: (i, j))
    if residual is None:
        return matmul(a, w, layer=layer, col0=col0, n=n, out_shapes=jax.ShapeDtypeStruct((m, n), out_dtype),
                      out_specs=tile, tm=tm, tn=tn, name=name)
    return matmul(a, w, layer=layer, col0=col0, n=n, epilogue=_ep_residual, extras=(residual,),
                  extra_specs=(tile,), out_shapes=jax.ShapeDtypeStruct((m, n), F32), out_specs=tile,
                  tm=tm, tn=tn, name=name)


def _mm2_residual_kernel(a1_ref, w1_ref, a2_ref, w2_ref, res_ref, o_ref):
    o_ref[...] = (res_ref[...]
                  + jnp.dot(a1_ref[...], w1_ref[...].astype(BF16), preferred_element_type=F32)
                  + jnp.dot(a2_ref[...], w2_ref[...].astype(BF16), preferred_element_type=F32))


def matmul2_residual(a1, a2, w, residual, name, *, layer):
    m, k1 = a1.shape
    k2 = a2.shape[1]
    n = w.shape[2]
    assert k1 == k2 and w.shape[1] == k1 + k2
    tm, tn = _pick(m, 1024), _pick(n, 512)
    tile = pl.BlockSpec((tm, tn), lambda i, j: (i, j))
    return pl.pallas_call(
        _mm2_residual_kernel,
        out_shape=jax.ShapeDtypeStruct((m, n), F32),
        grid=(m // tm, n // tn),
        in_specs=[pl.BlockSpec((tm, k1), lambda i, j: (i, 0)),
                  pl.BlockSpec((None, k1, tn), lambda i, j: (layer, 0, j)),
                  pl.BlockSpec((tm, k2), lambda i, j: (i, 0)),
                  pl.BlockSpec((None, k2, tn), lambda i, j: (layer, 1, j)),
                  tile],
        out_specs=tile,
        compiler_params=_cparams(("parallel", "parallel")),
        name=name,
    )(a1, w, a2, w, residual)


def _ret_kernel(lg_ref, gl_ref, q_ref, k_ref, v_ref, gate_ref, cos_ref, sin_ref, gn_ref, *rest,
                L, hd, has_s0, head_major):
    if has_s0:
        s0_ref, y_ref, sout_ref, s_scr = rest
    else:
        y_ref, sout_ref, s_scr = rest
    h = pl.program_id(1)
    c = pl.program_id(2)
    lg = lg_ref[h]

    @pl.when(c == 0)
    def _():
        if has_s0:
            s_scr[...] = s0_ref[0, 0]
        else:
            s_scr[...] = jnp.zeros_like(s_scr)

    half = hd // 2
    cos = cos_ref[...]
    sin = sin_ref[...]

    def rope(x):
        x1, x2 = x[:, :half], x[:, half:]
        return jnp.concatenate([x1 * cos - x2 * sin, x1 * sin + x2 * cos], axis=-1)

    def tile(ref):
        return ref[0, 0] if head_major else ref[...]

    q = rope(tile(q_ref))
    k = rope(tile(k_ref)) * (hd ** -0.5)
    vb = tile(v_ref).astype(BF16)
    n_col = lax.broadcasted_iota(jnp.int32, (L, 1), 0).astype(F32)
    diff = (lax.broadcasted_iota(jnp.int32, (L, L), 0)
            - lax.broadcasted_iota(jnp.int32, (L, L), 1)).astype(F32)
    d_in = jnp.where(diff >= 0, jnp.exp(diff * lg), 0.0)
    qb = q.astype(BF16)
    s = lax.dot_general(qb, k.astype(BF16), (((1,), (1,)), ((), ())),
                        preferred_element_type=F32) * d_in
    inner = jnp.dot(s.astype(BF16), vb, preferred_element_type=F32)
    s_old = s_scr[...]
    cross = jnp.dot(qb, s_old.astype(BF16), preferred_element_type=F32) * jnp.exp((n_col + 1.0) * lg)
    kd = (k * jnp.exp((L - 1.0 - n_col) * lg)).astype(BF16)
    s_new = gl_ref[h] * s_old + lax.dot_general(kd, vb, (((0,), (0,)), ((), ())),
                                                preferred_element_type=F32)
    s_scr[...] = s_new
    o = inner + cross
    xc = o - jnp.mean(o, axis=-1, keepdims=True)
    on = xc * lax.rsqrt(jnp.mean(xc * xc, axis=-1, keepdims=True) + EPS) * gn_ref[...]
    y_ref[...] = (jax.nn.silu(tile(gate_ref)) * on).astype(y_ref.dtype)

    @pl.when(c == pl.num_programs(2) - 1)
    def _():
        sout_ref[0, 0] = s_new


def retention(proj, cos, sin, gn_g, s0, *, B, T, H, hd):
    L = _pick(T, 256)
    nc = T // L
    lg = jnp.log1p(-jnp.exp2(-5.0 - jnp.arange(H, dtype=F32)))
    gl = jnp.exp(L * lg)
    has_s0 = s0 is not None
    head_major = proj.ndim == 4

    def col(sec):
        if head_major:
            return pl.BlockSpec((1, 1, L, hd), lambda b, h, c: (b, sec * H + h, c, 0))
        return pl.BlockSpec((L, hd), lambda b, h, c: (b * nc + c, sec * H + h))

    smem = pl.BlockSpec(memory_space=pltpu.SMEM)
    in_specs = [smem, smem, col(0), col(1), col(2), col(3),
                pl.BlockSpec((L, hd // 2), lambda b, h, c: (c, 0)),
                pl.BlockSpec((L, hd // 2), lambda b, h, c: (c, 0)),
                pl.BlockSpec((1, hd), lambda b, h, c: (0, h))]
    args = [lg, gl, proj, proj, proj, proj, cos, sin, gn_g.reshape(1, H * hd)]
    state_spec = pl.BlockSpec((1, 1, hd, hd), lambda b, h, c: (b, h, 0, 0))
    if has_s0:
        in_specs.append(state_spec)
        args.append(s0)
    return pl.pallas_call(
        functools.partial(_ret_kernel, L=L, hd=hd, has_s0=has_s0, head_major=head_major),
        out_shape=(jax.ShapeDtypeStruct((B * T, H * hd), BF16),
                   jax.ShapeDtypeStruct((B, H, hd, hd), F32)),
        grid=(B, H, nc),
        in_specs=in_specs,
        out_specs=(pl.BlockSpec((L, hd), lambda b, h, c: (b * nc + c, h)), state_spec),
        scratch_shapes=[pltpu.VMEM((hd, hd), F32)],
        compiler_params=_cparams(("parallel", "parallel", "arbitrary")),
        name="retention",
    )(*args)


def _cmlp_kernel(u_ref, vb_ref, lng_ref, lnb_ref, ws_ref, bst_ref, y_ref, *zv_out, G, gd, L):
    zu = jax.nn.gelu(u_ref[...])
    gv = jax.nn.gelu(vb_ref[...])
    xc = gv - jnp.mean(gv, axis=-1, keepdims=True)
    zv = xc * lax.rsqrt(jnp.mean(xc * xc, axis=-1, keepdims=True) + EPS) * lng_ref[...] + lnb_ref[...]
    if zv_out:
        zv_out[0][...] = zv
    keep = (lax.broadcasted_iota(jnp.int32, (L, L), 0) >= lax.broadcasted_iota(jnp.int32, (L, L), 1))
    for g in range(G):
        w = jnp.where(keep, ws_ref[g, :L, :L], 0.0).astype(BF16)
        mixed = jnp.dot(w, zv[:, g * gd:(g + 1) * gd].astype(BF16), preferred_element_type=F32)
        mixed = mixed + bst_ref[:, g:g + 1]
        y_ref[:, g * gd:(g + 1) * gd] = (zu[:, g * gd:(g + 1) * gd] * mixed).astype(y_ref.dtype)


def cmlp(proj, ln_g, ln_b, ws, bs, *, B, T, u_col, want_zv):
    G, chunk, _ = ws.shape
    W = ln_g.shape[0]
    gd = W // G
    L = min(T, chunk)
    nc = T // L
    out_shape = [jax.ShapeDtypeStruct((B * T, W), BF16)]
    out_specs = [pl.BlockSpec((L, W), lambda b, c: (b * nc + c, 0))]
    if want_zv:
        out_shape.append(jax.ShapeDtypeStruct((B * T, W), F32))
        out_specs.append(pl.BlockSpec((L, W), lambda b, c: (b * nc + c, 0)))
    res = pl.pallas_call(
        functools.partial(_cmlp_kernel, G=G, gd=gd, L=L),
        out_shape=tuple(out_shape),
        grid=(B, nc),
        in_specs=[pl.BlockSpec((L, W), lambda b, c: (b * nc + c, u_col)),
                  pl.BlockSpec((L, W), lambda b, c: (b * nc + c, u_col + 1)),
                  pl.BlockSpec((1, W), lambda b, c: (0, 0)),
                  pl.BlockSpec((1, W), lambda b, c: (0, 0)),
                  pl.BlockSpec((G, chunk, chunk), lambda b, c: (0, 0, 0)),
                  pl.BlockSpec((L, G), lambda b, c: (0, 0))],
        out_specs=tuple(out_specs),
        compiler_params=_cparams(("parallel", "parallel")),
        name="cmlp",
    )(proj, proj, ln_g.reshape(1, W), ln_b.reshape(1, W), ws, bs[:, :L].T)
    return res if want_zv else (res[0], None)


def _mem_kernel(x_ref, g_ref, wq_ref, qn_ref, mk_ref, mv_ref, wo_ref, o_ref, *, heads, hd):
    x = x_ref[...]
    h = _rms(x, g_ref[...]).astype(BF16)
    q = jnp.dot(h, wq_ref[...], preferred_element_type=F32)
    mk = mk_ref[0]
    mv = mv_ref[0]
    outs = []
    for hh in range(heads):
        sl = slice(hh * hd, (hh + 1) * hd)
        qh = _rms(q[:, sl], qn_ref[...]).astype(BF16)
        s = lax.dot_general(qh, mk[:, sl].astype(BF16), (((1,), (1,)), ((), ())),
                            preferred_element_type=F32) * (hd ** -0.5)
        e = jnp.exp(s - jnp.max(s, axis=-1, keepdims=True))
        p = (e / jnp.sum(e, axis=-1, keepdims=True)).astype(BF16)
        outs.append(jnp.dot(p, mv[:, sl].astype(BF16), preferred_element_type=F32))
    o = jnp.concatenate(outs, axis=-1).astype(BF16)
    o_ref[...] = x + jnp.dot(o, wo_ref[...], preferred_element_type=F32)


def mem_attn(x, g, wq, qn_g, mk, mv, wo, *, B, T, heads):
    n, d = x.shape
    mw = wq.shape[1]
    hd = mw // heads
    mlen = mk.shape[1]
    tm = _pick(T, 256)
    tpb = T // tm
    return pl.pallas_call(
        functools.partial(_mem_kernel, heads=heads, hd=hd),
        out_shape=jax.ShapeDtypeStruct((n, d), F32),
        grid=(n // tm,),
        in_specs=[pl.BlockSpec((tm, d), lambda i: (i, 0)),
                  pl.BlockSpec((1, d), lambda i: (0, 0)),
                  pl.BlockSpec((d, mw), lambda i: (0, 0)),
                  pl.BlockSpec((1, hd), lambda i: (0, 0)),
                  pl.BlockSpec((1, mlen, mw), lambda i: (i // tpb, 0, 0)),
                  pl.BlockSpec((1, mlen, mw), lambda i: (i // tpb, 0, 0)),
                  pl.BlockSpec((mw, d), lambda i: (0, 0))],
        out_specs=pl.BlockSpec((tm, d), lambda i: (i, 0)),
        compiler_params=_cparams(("parallel",)),
        name="mem_attn",
    )(x, g.reshape(1, d), wq, qn_g.reshape(1, hd), mk, mv, wo)


def _router_kernel(x_ref, g_ref, w_ref, b_ref, ids_ref, gates_ref, *, G, E):
    h = _rms(x_ref[...], g_ref[...])
    h1 = h.astype(BF16)
    h2 = (h - h1.astype(F32)).astype(BF16)
    a = jnp.dot(h1, w_ref[...], preferred_element_type=F32)
    logits = (a[:, :LANES] + a[:, LANES:] + jnp.dot(h2, w_ref[:, :LANES], preferred_element_type=F32)
              + b_ref[...])
    lane = lax.broadcasted_iota(jnp.int32, logits.shape, 1)
    neg = -jnp.inf
    is_g = lane < G
    lgm = jnp.where(is_g, logits, neg)
    gmax = jnp.max(lgm, axis=-1, keepdims=True)
    gsel = jnp.min(jnp.where(lgm == gmax, lane, LANES), axis=-1, keepdims=True)
    p_grp = 1.0 / jnp.sum(jnp.where(is_g, jnp.exp(logits - gmax), 0.0), axis=-1, keepdims=True)
    lo = G + gsel * E
    lem = jnp.where((lane >= lo) & (lane < lo + E), logits, neg)
    v1 = jnp.max(lem, axis=-1, keepdims=True)
    i1 = jnp.min(jnp.where(lem == v1, lane, LANES), axis=-1, keepdims=True)
    lem2 = jnp.where(lane == i1, neg, lem)
    v2 = jnp.max(lem2, axis=-1, keepdims=True)
    i2 = jnp.min(jnp.where(lem2 == v2, lane, LANES), axis=-1, keepdims=True)
    e2 = jnp.exp(v2 - v1)
    den = 1.0 + e2
    ids_ref[...] = jnp.where(lane == 0, i1 - G, jnp.where(lane == 1, i2 - G, 0))
    gates_ref[...] = jnp.where(lane == 0, p_grp / den, jnp.where(lane == 1, p_grp * e2 / den, 0.0))


def moe_router(x, g, w_rg, b_rg, w_re, b_re):
    n, d = x.shape
    G = w_rg.shape[1]
    E = w_re.shape[1] // G
    assert G + G * E <= LANES
    pad = LANES - G - G * E
    w = jnp.concatenate([w_rg, w_re, jnp.zeros((d, pad), F32)], axis=1)
    w1 = w.astype(BF16)
    w = jnp.concatenate([w1, (w - w1.astype(F32)).astype(BF16)], axis=1)
    b = jnp.concatenate([b_rg, b_re, jnp.zeros((pad,), F32)]).reshape(1, LANES)
    tm = _pick(n, 256)
    row = pl.BlockSpec((tm, LANES), lambda i: (i, 0))
    return pl.pallas_call(
        functools.partial(_router_kernel, G=G, E=E),
        out_shape=(jax.ShapeDtypeStruct((n, LANES), jnp.int32),
                   jax.ShapeDtypeStruct((n, LANES), F32)),
        grid=(n // tm,),
        in_specs=[pl.BlockSpec((tm, d), lambda i: (i, 0)),
                  pl.BlockSpec((1, d), lambda i: (0, 0)),
                  pl.BlockSpec((d, 2 * LANES), lambda i: (0, 0)),
                  pl.BlockSpec((1, LANES), lambda i: (0, 0))],
        out_specs=(row, row),
        compiler_params=_cparams(("parallel",)),
        name="moe_router",
    )(x, g.reshape(1, d), w, b)


def moe_dispatch(ids, n_experts, tm):
    n = ids.shape[0]
    flat = ids[:, :2].reshape(-1)
    experts = jnp.arange(n_experts, dtype=jnp.int32)
    hit = flat[:, None] == experts[None, :]
    counts = jnp.sum(hit.astype(jnp.int32), axis=0)
    padded = ((counts + tm - 1) // tm) * tm
    ends = jnp.cumsum(padded)
    starts = ends - padded
    raw_starts = jnp.cumsum(counts) - counts
    order = jnp.argsort(flat, stable=True).astype(jnp.int32)
    sorted_pos = jnp.argsort(order).astype(jnp.int32)
    pos = sorted_pos + jnp.sum(jnp.where(hit, (starts - raw_starts)[None, :], 0), axis=1)
    max_tiles = (2 * n) // tm + n_experts + FFN_GATHER_SLOTS - 1
    tile_start = jnp.arange(max_tiles, dtype=jnp.int32) * tm
    tile_expert = jnp.minimum(jnp.sum((ends[None, :] <= tile_start[:, None]).astype(jnp.int32), axis=1),
                              n_experts - 1)
    tile_src = jnp.minimum(raw_starts[tile_expert] + tile_start - starts[tile_expert], 2 * n)
    order = jnp.concatenate([order, jnp.arange(tm, dtype=jnp.int32) * 2])
    n_tiles = (ends[-1] // tm).astype(jnp.int32).reshape(1)
    return order, pos.astype(jnp.int32), tile_expert, tile_src.astype(jnp.int32), n_tiles


def _row_copy(src_hbm, dst, src_row, r, sem):
    return pltpu.make_async_copy(src_hbm.at[pl.ds(src_row, 1)], dst.at[pl.ds(r, 1)], sem)


def _row_gather_start(src_hbm, dst, idx_ref, base, rows, sem, shift=0):
    for r in range(rows):
        _row_copy(src_hbm, dst, idx_ref[base + r] >> shift, r, sem).start()


def _row_gather_wait(src_hbm, dst, rows, sem):
    for r in range(rows):
        _row_copy(src_hbm, dst, 0, r, sem).wait()


def _ffn_kernel(te_ref, ts_ref, order_ref, nt_ref, x_hbm, g_ref, wg_ref, wu_ref, wd_ref, y_ref,
                xbuf, hbuf, sem, *, tm):
    del te_ref
    i = pl.program_id(0)
    nt = nt_ref[0]
    slot = i % FFN_GATHER_SLOTS
    ahead = FFN_GATHER_SLOTS - 1

    def request(tile):
        s = tile % FFN_GATHER_SLOTS
        _row_gather_start(x_hbm, xbuf.at[s], order_ref, ts_ref[tile], tm, sem.at[s], shift=1)

    @pl.when(i == 0)
    def _():
        for t in range(ahead):
            request(t)

    @pl.when(i < nt)
    def _():
        _row_gather_wait(x_hbm, xbuf.at[slot], tm, sem.at[slot])
        hbuf[...] = _rms(xbuf[slot], g_ref[...]).astype(BF16)
        request(i + ahead)
        h = hbuf[...]
        a = (jax.nn.silu(jnp.dot(h, wg_ref[0], preferred_element_type=F32))
             * jnp.dot(h, wu_ref[0], preferred_element_type=F32))
        y_ref[...] = jnp.dot(a.astype(BF16), wd_ref[0], preferred_element_type=F32)

    @pl.when((i >= nt) & (i < nt + ahead))
    def _():
        _row_gather_wait(x_hbm, xbuf.at[slot], tm, sem.at[slot])

    @pl.when(i >= nt)
    def _():
        y_ref[...] = jnp.zeros_like(y_ref)


def moe_ffn(x, g, wg, wu, wd, order, tile_expert, tile_src, n_tiles, *, tm, first_expert):
    n, d = x.shape
    f = wg.shape[2]
    max_tiles = tile_expert.shape[0]
    grid_spec = pltpu.PrefetchScalarGridSpec(
        num_scalar_prefetch=4,
        grid=(max_tiles,),
        in_specs=[pl.BlockSpec(memory_space=pl.ANY),
                  pl.BlockSpec((1, d), lambda i, te, *_: (0, 0)),
                  pl.BlockSpec((1, d, f), lambda i, te, *_: (first_expert + te[i], 0, 0)),
                  pl.BlockSpec((1, d, f), lambda i, te, *_: (first_expert + te[i], 0, 0)),
                  pl.BlockSpec((1, f, d), lambda i, te, *_: (first_expert + te[i], 0, 0))],
        out_specs=pl.BlockSpec((tm, d), lambda i, te, *_: (i, 0)),
        scratch_shapes=[pltpu.VMEM((FFN_GATHER_SLOTS, tm, d), F32), pltpu.VMEM((tm, d), BF16),
                        pltpu.SemaphoreType.DMA((FFN_GATHER_SLOTS,))],
    )
    return pl.pallas_call(
        functools.partial(_ffn_kernel, tm=tm),
        out_shape=jax.ShapeDtypeStruct((max_tiles * tm, d), F32),
        grid_spec=grid_spec,
        compiler_params=_cparams(("arbitrary",)),
        name="moe_ffn",
    )(tile_expert, tile_src, order, n_tiles, x, g.reshape(1, d), wg, wu, wd)


def _combine_kernel(pos_ref, x_ref, gates_ref, y_hbm, *rest, tc, with_norm):
    if with_norm:
        g_ref, o_ref, h_ref, ybuf, sem = rest
    else:
        o_ref, ybuf, sem = rest
    i = pl.program_id(0)
    nsteps = pl.num_programs(0)
    slot = i % 2

    @pl.when(i == 0)
    def _():
        _row_gather_start(y_hbm, ybuf.at[0], pos_ref, 0, 2 * tc, sem.at[0])

    @pl.when(i + 1 < nsteps)
    def _():
        _row_gather_start(y_hbm, ybuf.at[1 - slot], pos_ref, (i + 1) * 2 * tc, 2 * tc, sem.at[1 - slot])

    _row_gather_wait(y_hbm, ybuf.at[slot], 2 * tc, sem.at[slot])
    gates = gates_ref[...]
    out = (x_ref[...] + gates[:, 0:1] * ybuf[slot, pl.ds(0, tc)]
           + gates[:, 1:2] * ybuf[slot, pl.ds(tc, tc)])
    o_ref[...] = out
    if with_norm:
        h_ref[...] = _rms(out, g_ref[...]).astype(h_ref.dtype)


def moe_combine(x, gates, y_rows, pos, next_g, *, tc):
    n, d = x.shape
    nt = n // tc
    with_norm = next_g is not None
    pos_tiled = pos.reshape(nt, tc, 2).transpose(0, 2, 1).reshape(-1)
    row = pl.BlockSpec((tc, d), lambda i, p: (i, 0))
    in_specs = [row, pl.BlockSpec((tc, LANES), lambda i, p: (i, 0)), pl.BlockSpec(memory_space=pl.ANY)]
    args = [pos_tiled, x, gates, y_rows]
    out_shape = jax.ShapeDtypeStruct((n, d), F32)
    out_specs = row
    if with_norm:
        in_specs.append(pl.BlockSpec((1, d), lambda i, p: (0, 0)))
        args.append(next_g.reshape(1, d))
        out_shape = (out_shape, jax.ShapeDtypeStruct((n, d), BF16))
        out_specs = (row, row)
    grid_spec = pltpu.PrefetchScalarGridSpec(
        num_scalar_prefetch=1,
        grid=(nt,),
        in_specs=in_specs,
        out_specs=out_specs,
        scratch_shapes=[pltpu.VMEM((2, 2 * tc, d), F32), pltpu.SemaphoreType.DMA((2,))],
    )
    res = pl.pallas_call(
        functools.partial(_combine_kernel, tc=tc, with_norm=with_norm),
        out_shape=out_shape,
        grid_spec=grid_spec,
        compiler_params=_cparams(("arbitrary",)),
        name="moe_combine",
    )(*args)
    return res if with_norm else (res, None)


def hier_moe(x, g, w_rg, b_rg, w_re, b_re, wg, wu, wd, *, first_expert, n_experts, next_g):
    n = x.shape[0]
    mean_load = max(1, 2 * n // n_experts)
    tm = min(256, max(32, 1 << (mean_load - 1).bit_length()))
    assert tm <= n
    ids, gates = moe_router(x, g, w_rg, b_rg, w_re, b_re)
    order, pos, tile_expert, tile_src, n_tiles = moe_dispatch(ids, n_experts, tm)
    y_rows = moe_ffn(x, g, wg, wu, wd, order, tile_expert, tile_src, n_tiles, tm=tm,
                     first_expert=first_expert)
    return moe_combine(x, gates, y_rows, pos, next_g, tc=_pick(n, 128))


def _cumsum_kernel(x_ref, c_ref, ct_ref, carry, *, L):
    @pl.when(pl.program_id(1) == 0)
    def _():
        carry[...] = jnp.zeros_like(carry)

    tri = (lax.broadcasted_iota(jnp.int32, (L, L), 0)
           >= lax.broadcasted_iota(jnp.int32, (L, L), 1)).astype(F32)
    c = jnp.dot(tri, x_ref[0], preferred_element_type=F32,
                precision=lax.Precision.HIGHEST) + carry[...]
    c_ref[0] = c * LOG2E
    ct_ref[0] = (c * LOG2E).T
    carry[...] = c[L - 1:L, :]


def cumsum_time(x):
    B, T, w = x.shape
    L = _pick(T, 256)
    return pl.pallas_call(
        functools.partial(_cumsum_kernel, L=L),
        out_shape=(jax.ShapeDtypeStruct((B, T, w), F32), jax.ShapeDtypeStruct((B, w, T), F32)),
        grid=(B, T // L),
        in_specs=[pl.BlockSpec((1, L, w), lambda b, t: (b, t, 0))],
        out_specs=(pl.BlockSpec((1, L, w), lambda b, t: (b, t, 0)),
                   pl.BlockSpec((1, w, L), lambda b, t: (b, 0, t))),
        scratch_shapes=[pltpu.VMEM((1, w), F32)],
        compiler_params=_cparams(("parallel", "arbitrary")),
        name="cumsum_time",
    )(x)


def _row_bcast(r, width):
    if width % LANES == 0:
        return jnp.tile(r, (1, width // LANES))
    return r[:, :1]


def _softmax_stats(t, cq, m_old, l_old):
    m_new = jnp.maximum(m_old, jnp.max(t, axis=-1, keepdims=True) + cq)
    alpha = jnp.exp2(m_old - m_new)
    p = jnp.exp2(t + _row_bcast(cq - m_new, t.shape[1]))
    l_new = alpha * l_old + jnp.sum(p, axis=-1, keepdims=True)
    return m_new, l_new, alpha, p


def _fox_prompt_kernel(q_ref, k_ref, v_ref, cq_ref, ck_ref, o_ref, m_scr, l_scr, acc_scr, cq_scr,
                       *, HG, tq, tk, hd):
    hg = pl.program_id(1)
    qi = pl.program_id(2)
    ki = pl.program_id(3)

    @pl.when(ki == 0)
    def _():
        m_scr[...] = jnp.full_like(m_scr, -jnp.inf)
        l_scr[...] = jnp.zeros_like(l_scr)
        acc_scr[...] = jnp.zeros_like(acc_scr)
        cq_all = cq_ref[...]
        lane = lax.broadcasted_iota(jnp.int32, cq_all.shape, 1)
        for hh in range(HG):
            col = jnp.sum(jnp.where(lane == hg * HG + hh, cq_all, 0.0), axis=-1, keepdims=True)
            cq_scr[hh] = jnp.broadcast_to(col, cq_all.shape)

    def run(masked):
        if masked:
            keep = (lax.broadcasted_iota(jnp.int32, (tq, tk), 0) + qi * tq
                    >= lax.broadcasted_iota(jnp.int32, (tq, tk), 1) + ki * tk)

        def head(hh, carry):
            s = lax.dot_general(q_ref[0, hh], k_ref[0, hh], (((1,), (1,)), ((), ())),
                                preferred_element_type=F32)
            t = s - ck_ref[0, pl.ds(hh, 1), :]
            if masked:
                t = jnp.where(keep, t, -jnp.inf)
            m_new, l_new, alpha, p = _softmax_stats(t, cq_scr[hh], m_scr[hh], l_scr[hh])
            m_scr[hh] = m_new
            l_scr[hh] = l_new
            acc_scr[hh] = alpha * acc_scr[hh] + jnp.dot(p.astype(BF16), v_ref[0, hh],
                                                        preferred_element_type=F32)
            return carry

        lax.fori_loop(0, HG, head, 0, unroll=4)

    @pl.when(ki < qi)
    def _():
        run(False)

    @pl.when(ki == qi)
    def _():
        run(True)
        for hh in range(HG):
            o_ref[:, hh * hd:(hh + 1) * hd] = (acc_scr[hh] / l_scr[hh]).astype(o_ref.dtype)


def fox_prompt(q_hm, k_hm, v_hm, c_tok, c_t, *, B, T, H, hd):
    HG = _pick(H, 16)
    tq = tk = _pick(T, 512)
    nq = T // tq
    kv_spec = pl.BlockSpec((1, HG, tk, hd), lambda b, g, qi, ki: (b, g, jnp.minimum(ki, qi), 0))
    return pl.pallas_call(
        functools.partial(_fox_prompt_kernel, HG=HG, tq=tq, tk=tk, hd=hd),
        out_shape=jax.ShapeDtypeStruct((B * T, H * hd), BF16),
        grid=(B, H // HG, nq, nq),
        in_specs=[pl.BlockSpec((1, HG, tq, hd), lambda b, g, qi, ki: (b, g, qi, 0)),
                  kv_spec, kv_spec,
                  pl.BlockSpec((tq, LANES), lambda b, g, qi, ki: (b * nq + qi, 0)),
                  pl.BlockSpec((1, HG, tk), lambda b, g, qi, ki: (b, g, jnp.minimum(ki, qi)))],
        out_specs=pl.BlockSpec((tq, HG * hd), lambda b, g, qi, ki: (b * nq + qi, g)),
        scratch_shapes=[pltpu.VMEM((HG, tq, LANES), F32), pltpu.VMEM((HG, tq, LANES), F32),
                        pltpu.VMEM((HG, tq, hd), F32), pltpu.VMEM((HG, tq, LANES), F32)],
        compiler_params=_cparams(("parallel", "parallel", "parallel", "arbitrary")),
        name="fox_prompt",
    )(q_hm, k_hm, v_hm, c_tok, c_t)


def _fox_sample_kernel(q_ref, kn_ref, vn_ref, k_hbm, v_hbm, cq_ref, cn_ref, cp_ref, o_ref,
                       cq_scr, m_scr, l_scr, acc_scr, s_scr, p_scr, kbuf, vbuf, sem, *, H, hd, Ts, tk, layer):
    b = pl.program_id(0)
    j = pl.program_id(1)

    def tile_copies(tile, slot):
        span = pl.ds(tile * tk, tk)
        for hh in range(H):
            yield pltpu.make_async_copy(k_hbm.at[layer, b, span, hh, :], kbuf.at[slot, hh], sem.at[slot, 0])
            yield pltpu.make_async_copy(v_hbm.at[layer, b, span, hh, :], vbuf.at[slot, hh], sem.at[slot, 1])

    def rows(hh):
        return slice(hh * Ts, (hh + 1) * Ts)

    def cols(hh):
        return slice(hh * hd, (hh + 1) * hd)

    def update(width, key, value, ck, keep):
        for hh in range(H):
            s = lax.dot_general(q_ref[:, cols(hh)], key(hh), (((1,), (1,)), ((), ())),
                                preferred_element_type=F32)
            t = s - ck(hh)
            s_scr[rows(hh), :width] = t if keep is None else jnp.where(keep, t, -jnp.inf)
        m_new, l_new, alpha, p = _softmax_stats(s_scr[:, :width], cq_scr[...], m_scr[...], l_scr[...])
        m_scr[...] = m_new
        l_scr[...] = l_new
        p_scr[:, :width] = p.astype(BF16)
        for hh in range(H):
            acc_scr[rows(hh)] = alpha[rows(hh)] * acc_scr[rows(hh)] + jnp.dot(
                p_scr[rows(hh), :width], value(hh), preferred_element_type=F32)

    @pl.when(j == 0)
    def _():
        for cp in tile_copies(0, 0):
            cp.start()
        m_scr[...] = jnp.full_like(m_scr, -jnp.inf)
        l_scr[...] = jnp.zeros_like(l_scr)
        acc_scr[...] = jnp.zeros_like(acc_scr)
        for hh in range(H):
            cq_scr[rows(hh)] = jnp.broadcast_to(cq_ref[0, :, hh:hh + 1], (Ts, LANES))
        keep = (lax.broadcasted_iota(jnp.int32, (Ts, Ts), 0) >= lax.broadcasted_iota(jnp.int32, (Ts, Ts), 1))
        update(Ts, lambda hh: kn_ref[:, cols(hh)].astype(BF16), lambda hh: vn_ref[:, cols(hh)].astype(BF16),
               lambda hh: cn_ref[0, hh:hh + 1, :Ts], keep)

    @pl.when(j > 0)
    def _():
        tile = j - 1
        slot = tile % 2
        for cp in tile_copies(tile, slot):
            cp.wait()

        @pl.when(j < pl.num_programs(1) - 1)
        def _():
            for cp in tile_copies(tile + 1, 1 - slot):
                cp.start()

        update(tk, lambda hh: kbuf[slot, hh].astype(BF16), lambda hh: vbuf[slot, hh].astype(BF16),
               lambda hh: cp_ref[0, hh:hh + 1, :], None)

    @pl.when(j == pl.num_programs(1) - 1)
    def _():
        for hh in range(H):
            o_ref[:, cols(hh)] = (acc_scr[rows(hh)] / l_scr[rows(hh)]).astype(o_ref.dtype)


def fox_sample(q, k_new, v_new, k_past, v_past, c_tok, c_t, *, B, Ts, P, H, hd, layer):
    w = H * hd
    tk = _pick(P, 256)
    npk = P // tk
    assert P % LANES == 0 and Ts <= LANES and P % Ts == 0 and hd == LANES
    past = pl.BlockSpec(memory_space=pl.ANY)
    new = pl.BlockSpec((Ts, w), lambda b, j: (b, 0))
    return pl.pallas_call(
        functools.partial(_fox_sample_kernel, H=H, hd=hd, Ts=Ts, tk=tk, layer=layer),
        out_shape=jax.ShapeDtypeStruct((B * Ts, w), BF16),
        grid=(B, npk + 1),
        in_specs=[new, new, new, past, past,
                  pl.BlockSpec((1, Ts, LANES), lambda b, j: (b, P // Ts, 0)),
                  pl.BlockSpec((1, LANES, LANES), lambda b, j: (b, 0, P // LANES)),
                  pl.BlockSpec((1, LANES, tk), lambda b, j: (b, 0, jnp.maximum(j - 1, 0)))],
        out_specs=new,
        scratch_shapes=[pltpu.VMEM((H * Ts, LANES), F32), pltpu.VMEM((H * Ts, LANES), F32),
                        pltpu.VMEM((H * Ts, LANES), F32), pltpu.VMEM((H * Ts, hd), F32),
                        pltpu.VMEM((H * Ts, tk), F32), pltpu.VMEM((H * Ts, tk), BF16),
                        pltpu.VMEM((2, H, tk, hd), F32), pltpu.VMEM((2, H, tk, hd), F32),
                        pltpu.SemaphoreType.DMA((2, 2))],
        compiler_params=_cparams(("arbitrary", "arbitrary")),
        name="fox_sample",
    )(q, k_new, v_new, k_past, v_past, c_tok, c_t, c_t)


def fox_project(h, w_in, layer, wf, b_f, qn_g, kn_g, *, B, T, H, hd, head_major):
    m, _ = h.shape
    w = H * hd
    tm = _pick(T, 1024) if head_major else _pick(m, 1024)
    tn = _pick(w, 512)
    tpb = T // tm if head_major else 1
    tok = pl.BlockSpec((tm, tn), lambda i, j: (i, j))
    hm = pl.BlockSpec((1, tn // hd, tm, hd), lambda i, j: (i // tpb, j, i % tpb, 0))
    gain = pl.BlockSpec((1, hd), lambda i, j: (0, 0))
    tok_f32 = jax.ShapeDtypeStruct((m, w), F32)
    tok_bf16 = jax.ShapeDtypeStruct((m, w), BF16)
    hm_bf16 = jax.ShapeDtypeStruct((B, H, T, hd), BF16)

    def proj(section, g, norm, scale, shapes, specs, kinds, name):
        ep = functools.partial(_ep_heads, hd=hd, norm=norm, scale=scale, kinds=kinds)
        extras = (g.reshape(1, hd),) if norm else ()
        especs = (gain,) if norm else ()
        return matmul(h, w_in, layer=layer, col0=section * w, n=w, epilogue=ep, extras=extras,
                      extra_specs=especs, out_shapes=shapes, out_specs=specs, tm=tm, tn=tn, name=name)

    q_scale = hd ** -0.5 * LOG2E
    if head_major:
        (q,) = proj(0, qn_g, True, q_scale, (hm_bf16,), (hm,), ("head",), "fox_q")
        k, k_hm = proj(1, kn_g, True, 1.0, (tok_f32, hm_bf16), (tok, hm), ("tok", "head"), "fox_k")
        v, v_hm = proj(2, None, False, 1.0, (tok_f32, hm_bf16), (tok, hm), ("tok", "head"), "fox_v")
    else:
        (q,) = proj(0, qn_g, True, q_scale, (tok_bf16,), (tok,), ("tok",), "fox_q")
        (k,) = proj(1, kn_g, True, 1.0, (tok_f32,), (tok,), ("tok",), "fox_k")
        (v,) = proj(2, None, False, 1.0, (tok_f32,), (tok,), ("tok",), "fox_v")
        k_hm = v_hm = None
    tmf = _pick(m, 1024)
    logf = matmul(h, wf, layer=0, n=LANES, epilogue=_ep_logsig, extras=(b_f,),
                  extra_specs=(pl.BlockSpec((1, LANES), lambda i, j: (0, 0)),),
                  out_shapes=jax.ShapeDtypeStruct((m, LANES), F32),
                  out_specs=pl.BlockSpec((tmf, LANES), lambda i, j: (i, 0)),
                  tm=tmf, tn=LANES, name="fox_logf")
    return q, k, v, logf, k_hm, v_hm


def _stack(parts):
    return parts[0][None] if len(parts) == 1 else jnp.stack(parts)


def _rope_tables(pos, hd):
    half = hd // 2
    inv = ROPE_BASE ** (-jnp.arange(half, dtype=F32) / half)
    ang = pos.astype(F32)[:, None] * inv[None, :]
    return jnp.cos(ang), jnp.sin(ang)


def kernel(x_prompt, x_sample, mem_prompt, state_ret, cache_fox_k, cache_fox_v, cache_fox_logf, cache_mem_k, cache_mem_v, norm_mix, norm_mem, norm_moe, even_w_in, ret_gn_g, cmlp_ln_g, cmlp_ln_b, cmlp_ws, cmlp_bs, even_w_out, odd_w_in, fox_b_f, fox_qn_g, fox_kn_g, odd_w_out, mem_src_g, mem_w_q, mem_w_k, mem_w_v, mem_qn_g, mem_kn_g, mem_w_o, moe_w_rg, moe_b_rg, moe_w_re, moe_b_re, moe_w_gate, moe_w_up, moe_w_down):
    Bp, Tp, D = x_prompt.shape
    Bs, Ts, _ = x_sample.shape
    depth = norm_mix.shape[0]
    P = cache_fox_k.shape[2]
    RH, RHD = state_ret.shape[2], state_ret.shape[3]
    RW = RH * RHD
    CW = cmlp_ln_g.shape[1]
    FH, FHD = cache_fox_k.shape[3], cache_fox_k.shape[4]
    FW = FH * FHD
    MH, MHD = cache_mem_k.shape[3], cache_mem_k.shape[4]
    MLEN = mem_prompt.shape[1]
    NG, NE, _, DF = moe_w_gate.shape[1:]
    assert (4 * RW) % CW == 0 and FHD == LANES and FH <= LANES

    xp = x_prompt.reshape(Bp * Tp, D)
    xs = x_sample.reshape(Bs * Ts, D)
    mem2d = mem_prompt.reshape(Bp * MLEN, D)
    cos_p, sin_p = _rope_tables(jnp.arange(Tp), RHD)
    cos_s, sin_s = _rope_tables(P + jnp.arange(Ts), RHD)

    wg_all = moe_w_gate.reshape(depth * NG * NE, D, DF).astype(BF16)
    wu_all = moe_w_up.reshape(depth * NG * NE, D, DF).astype(BF16)
    wd_all = moe_w_down.reshape(depth * NG * NE, DF, D).astype(BF16)

    ret_S_p, ret_S_s, cmlp_v_s = [], [], []
    fk_p, fv_p, fl_p, fk_s, fv_s, fl_s = [], [], [], [], [], []
    mk_p, mv_p = [], []
    for i in range(depth):
        j = i // 2
        if i == 0:
            hp = rmsnorm_rows(xp, norm_mix[i])
            hs = rmsnorm_rows(xs, norm_mix[i])
        if i % 2 == 0:
            def even(h, x, cos, sin, s0, B, T, want_zv, head_major):
                if head_major:
                    tm, tn = _pick(T, 1024), _pick(4 * RW, 512)
                    tpb = T // tm
                    (proj_ret,) = matmul(
                        h, even_w_in, layer=j, n=4 * RW,
                        epilogue=functools.partial(_ep_heads, hd=RHD, norm=False, scale=1.0, kinds=("head",)),
                        out_shapes=(jax.ShapeDtypeStruct((B, 4 * RH, T, RHD), F32),),
                        out_specs=(pl.BlockSpec((1, tn // RHD, tm, RHD),
                                                lambda a, b: (a // tpb, b, a % tpb, 0)),),
                        tm=tm, tn=tn, name="even_in_ret")
                else:
                    proj_ret = matmul_plain(h, even_w_in, F32, "even_in_ret", layer=j, n=4 * RW)
                proj_mlp = matmul_plain(h, even_w_in, F32, "even_in_mlp", layer=j, col0=4 * RW)
                y_a, S = retention(proj_ret, cos, sin, ret_gn_g[j], s0, B=B, T=T, H=RH, hd=RHD)
                y_b, zv = cmlp(proj_mlp, cmlp_ln_g[j], cmlp_ln_b[j], cmlp_ws[j], cmlp_bs[j],
                               B=B, T=T, u_col=0, want_zv=want_zv)
                return matmul2_residual(y_a, y_b, even_w_out, x, "even_out", layer=j), S, zv

            xp, Sp, _ = even(hp, xp, cos_p, sin_p, None, Bp, Tp, False, True)
            xs, Ss, zvs = even(hs, xs, cos_s, sin_s, state_ret[j], Bs, Ts, True, False)
            ret_S_p.append(Sp)
            ret_S_s.append(Ss)
            cmlp_v_s.append(zvs.reshape(Bs, Ts, CW))
        else:
            wf = jnp.pad(odd_w_in[j, :, 3 * FW:], ((0, 0), (0, LANES - FH)))[None]
            b_f = jnp.pad(fox_b_f[j], (0, LANES - FH)).reshape(1, LANES)
            q, k, v, logf, k_hm, v_hm = fox_project(hp, odd_w_in, j, wf, b_f, fox_qn_g[j], fox_kn_g[j],
                                                    B=Bp, T=Tp, H=FH, hd=FHD, head_major=True)
            c_tok, c_t = cumsum_time(logf.reshape(Bp, Tp, LANES))
            o = fox_prompt(q, k_hm, v_hm, c_tok.reshape(Bp * Tp, LANES), c_t[:, :FH], B=Bp, T=Tp, H=FH, hd=FHD)
            xp = matmul_plain(o, odd_w_out, F32, "odd_out", layer=j, residual=xp)
            fk_p.append(k.reshape(Bp, Tp, FH, FHD))
            fv_p.append(v.reshape(Bp, Tp, FH, FHD))
            fl_p.append(logf[:, :FH].reshape(Bp, Tp, FH))
            q, k, v, logf, _, _ = fox_project(hs, odd_w_in, j, wf, b_f, fox_qn_g[j], fox_kn_g[j],
                                              B=Bs, T=Ts, H=FH, hd=FHD, head_major=False)
            t_pad = -(-(P + Ts) // 256) * 256
            seq = jnp.concatenate([jnp.pad(cache_fox_logf[j], ((0, 0), (0, 0), (0, LANES - FH))),
                                   logf.reshape(Bs, Ts, LANES),
                                   jnp.zeros((Bs, t_pad - P - Ts, LANES), F32)], axis=1)
            c_tok, c_t = cumsum_time(seq)
            o = fox_sample(q, k, v, cache_fox_k, cache_fox_v, c_tok, c_t,
                           B=Bs, Ts=Ts, P=P, H=FH, hd=FHD, layer=j)
            xs = matmul_plain(o, odd_w_out, F32, "odd_out", layer=j, residual=xs)
            fk_s.append(k.reshape(Bs, Ts, FH, FHD))
            fv_s.append(v.reshape(Bs, Ts, FH, FHD))
            fl_s.append(logf[:, :FH].reshape(Bs, Ts, FH))
        m_n = rmsnorm_rows(mem2d, mem_src_g[i])
        mw = MH * MHD
        tmm, tnm = _pick(Bp * MLEN, 1024), _pick(mw, 512)
        tile = pl.BlockSpec((tmm, tnm), lambda a, b: (a, b))
        (mk,) = matmul(m_n, mem_w_k, layer=i, n=mw,
                       epilogue=functools.partial(_ep_heads, hd=MHD, norm=True, scale=1.0, kinds=("tok",)),
                       extras=(mem_kn_g[i].reshape(1, MHD),),
                       extra_specs=(pl.BlockSpec((1, MHD), lambda a, b: (0, 0)),),
                       out_shapes=(jax.ShapeDtypeStruct((Bp * MLEN, mw), F32),), out_specs=(tile,),
                       tm=tmm, tn=tnm, name="mem_k")
        mv = matmul_plain(m_n, mem_w_v, F32, "mem_v", layer=i)
        mk_p.append(mk.reshape(Bp, MLEN, MH, MHD))
        mv_p.append(mv.reshape(Bp, MLEN, MH, MHD))
        wq_m = mem_w_q[i].astype(BF16)
        wo_m = mem_w_o[i].astype(BF16)
        xp = mem_attn(xp, norm_mem[i], wq_m, mem_qn_g[i], mk.reshape(Bp, MLEN, mw), mv.reshape(Bp, MLEN, mw),
                      wo_m, B=Bp, T=Tp, heads=MH)
        xs = mem_attn(xs, norm_mem[i], wq_m, mem_qn_g[i], cache_mem_k[i].reshape(Bs, MLEN, mw),
                      cache_mem_v[i].reshape(Bs, MLEN, mw), wo_m, B=Bs, T=Ts, heads=MH)
        moe = (norm_moe[i], moe_w_rg[i], moe_b_rg[i], moe_w_re[i], moe_b_re[i], wg_all, wu_all, wd_all)
        next_g = norm_mix[i + 1] if i + 1 < depth else None
        xp, hp = hier_moe(xp, *moe, first_expert=i * NG * NE, n_experts=NG * NE, next_g=next_g)
        xs, hs = hier_moe(xs, *moe, first_expert=i * NG * NE, n_experts=NG * NE, next_g=next_g)

    return (xp.reshape(Bp, Tp, D), xs.reshape(Bs, Ts, D),
            _stack(ret_S_p), _stack(ret_S_s), _stack(cmlp_v_s),
            _stack(fk_p), _stack(fv_p), _stack(fl_p),
            _stack(fk_s), _stack(fv_s), _stack(fl_s),
            _stack(mk_p), _stack(mv_p))
```

```python
import functools

import jax
import jax.numpy as jnp
from jax import lax
from jax.experimental import pallas as pl
from jax.experimental.pallas import tpu as pltpu

F32 = jnp.float32
BF16 = jnp.bfloat16
EPS = 1e-6
ROPE_BASE = 10000.0
LOG2E = 1.4426950408889634
LANES = 128
VMEM_LIMIT_BYTES = 56 * 1024 * 1024
FFN_GATHER_SLOTS = 3


def _cparams(semantics):
    return pltpu.CompilerParams(dimension_semantics=semantics,
                                vmem_limit_bytes=VMEM_LIMIT_BYTES)


def _pick(n, pref):
    t = min(n, pref)
    while n % t:
        t //= 2
    assert t >= 1
    return t


def _rms(x, g):
    return x * lax.rsqrt(jnp.mean(x * x, axis=-1, keepdims=True) + EPS) * g


def _rmsnorm_kernel(x_ref, g_ref, o_ref):
    o_ref[...] = _rms(x_ref[...], g_ref[...]).astype(o_ref.dtype)


def rmsnorm_rows(x, g, out_dtype=BF16):
    n, d = x.shape
    tm = _pick(n, 512)
    return pl.pallas_call(
        _rmsnorm_kernel,
        out_shape=jax.ShapeDtypeStruct((n, d), out_dtype),
        grid=(n // tm,),
        in_specs=[pl.BlockSpec((tm, d), lambda i: (i, 0)),
                  pl.BlockSpec((1, d), lambda i: (0, 0))],
        out_specs=pl.BlockSpec((tm, d), lambda i: (i, 0)),
        compiler_params=_cparams(("parallel",)),
        name="rmsnorm_rows",
    )(x, g.reshape(1, d))


def _mm_kernel(*refs, n_extra, epilogue):
    a_ref, w_ref = refs[0], refs[1]
    extra = refs[2:2 + n_extra]
    outs = refs[2 + n_extra:]
    acc = jnp.dot(a_ref[...], w_ref[...].astype(BF16), preferred_element_type=F32)
    epilogue(acc, extra, outs)


def _ep_store(acc, extra, outs):
    outs[0][...] = acc.astype(outs[0].dtype)


def _ep_residual(acc, extra, outs):
    outs[0][...] = extra[0][...] + acc


def _ep_logsig(acc, extra, outs):
    outs[0][...] = jax.nn.log_sigmoid(acc + extra[0][...])


def _ep_heads(acc, extra, outs, *, hd, norm, scale, kinds):
    tn = acc.shape[1]
    for hh in range(tn // hd):
        blk = acc[:, hh * hd:(hh + 1) * hd]
        if norm:
            blk = _rms(blk, extra[0][...])
        if scale != 1.0:
            blk = blk * scale
        for o_ref, kind in zip(outs, kinds):
            if kind == "tok":
                o_ref[:, hh * hd:(hh + 1) * hd] = blk.astype(o_ref.dtype)
            else:
                o_ref[0, hh] = blk.astype(o_ref.dtype)


def _mm_tiles(m, n):
    tm = _pick(m, 1024)
    return tm, _pick(n, 512 if m > tm or tm > 256 else 1024)


def matmul(a, w, *, layer, col0=0, n, epilogue=_ep_store, extras=(), extra_specs=(), out_shapes,
           out_specs, tm, tn, name):
    m, k = a.shape
    assert w.shape[1] == k and m % tm == 0 and n % tn == 0 and col0 % tn == 0
    kern = functools.partial(_mm_kernel, n_extra=len(extras), epilogue=epilogue)
    return pl.pallas_call(
        kern,
        out_shape=out_shapes,
        grid=(m // tm, n // tn),
        in_specs=[pl.BlockSpec((tm, k), lambda i, j: (i, 0)),
                  pl.BlockSpec((None, k, tn), lambda i, j: (layer, 0, j + col0 // tn))] + list(extra_specs),
        out_specs=out_specs,
        compiler_params=_cparams(("parallel", "parallel")),
        name=name,
    )(a, w, *extras)


def matmul_plain(a, w, out_dtype, name, *, layer, col0=0, n=None, residual=None):
    m, _ = a.shape
    n = w.shape[2] - col0 if n is None else n
    tm, tn = _mm_tiles(m, n)
    tile = pl.BlockSpec((tm, tn), lambda i, j: (i, j))
    if residual is None:
        return matmul(a, w, layer=layer, col0=col0, n=n, out_shapes=jax.ShapeDtypeStruct((m, n), out_dtype),
                      out_specs=tile, tm=tm, tn=tn, name=name)
    return matmul(a, w, layer=layer, col0=col0, n=n, epilogue=_ep_residual, extras=(residual,),
                  extra_specs=(tile,), out_shapes=jax.ShapeDtypeStruct((m, n), F32), out_specs=tile,
                  tm=tm, tn=tn, name=name)


def _mm2_residual_kernel(a1_ref, w1_ref, a2_ref, w2_ref, res_ref, o_ref):
    o_ref[...] = (res_ref[...]
                  + jnp.dot(a1_ref[...], w1_ref[...].astype(BF16), preferred_element_type=F32)
                  + jnp.dot(a2_ref[...], w2_ref[...].astype(BF16), preferred_element_type=F32))


def matmul2_residual(a1, a2, w, residual, name, *, layer):
    m, k1 = a1.shape
    k2 = a2.shape[1]
    n = w.shape[2]
    assert k1 == k2 and w.shape[1] == k1 + k2
    tm, tn = _mm_tiles(m, n)
    tile = pl.BlockSpec((tm, tn), lambda i, j: (i, j))
    return pl.pallas_call(
        _mm2_residual_kernel,
        out_shape=jax.ShapeDtypeStruct((m, n), F32),
        grid=(m // tm, n // tn),
        in_specs=[pl.BlockSpec((tm, k1), lambda i, j: (i, 0)),
                  pl.BlockSpec((None, k1, tn), lambda i, j: (layer, 0, j)),
                  pl.BlockSpec((tm, k2), lambda i, j: (i, 0)),
                  pl.BlockSpec((None, k2, tn), lambda i, j: (layer, 1, j)),
                  tile],
        out_specs=tile,
        compiler_params=_cparams(("parallel", "parallel")),
        name=name,
    )(a1, w, a2, w, residual)


def _ret_kernel(lg_ref, gl_ref, q_ref, k_ref, v_ref, gate_ref, cos_ref, sin_ref, gn_ref, *rest,
                L, hd, has_s0, head_major):
    if has_s0:
        s0_ref, y_ref, sout_ref, s_scr, din_scr, qdec_scr, kdec_scr = rest
    else:
        y_ref, sout_ref, s_scr, din_scr, qdec_scr, kdec_scr = rest
    h = pl.program_id(1)
    c = pl.program_id(2)

    @pl.when(c == 0)
    def _():
        if has_s0:
            s_scr[...] = s0_ref[0, 0]
        else:
            s_scr[...] = jnp.zeros_like(s_scr)
        lg = lg_ref[h]
        n_col = lax.broadcasted_iota(jnp.int32, (L, 1), 0).astype(F32)
        diff = (lax.broadcasted_iota(jnp.int32, (L, L), 0)
                - lax.broadcasted_iota(jnp.int32, (L, L), 1)).astype(F32)
        din_scr[...] = jnp.where(diff >= 0, jnp.exp(diff * lg), 0.0)
        qdec_scr[...] = jnp.broadcast_to(jnp.exp((n_col + 1.0) * lg), (L, hd))
        kdec_scr[...] = jnp.broadcast_to(jnp.exp((L - 1.0 - n_col) * lg), (L, hd))

    half = hd // 2
    cos = cos_ref[...]
    sin = sin_ref[...]

    def rope(x):
        x1, x2 = x[:, :half], x[:, half:]
        return jnp.concatenate([x1 * cos - x2 * sin, x1 * sin + x2 * cos], axis=-1)

    def tile(ref):
        return ref[0, 0] if head_major else ref[...]

    q = rope(tile(q_ref))
    k = rope(tile(k_ref)) * (hd ** -0.5)
    vb = tile(v_ref).astype(BF16)
    qb = q.astype(BF16)
    s = lax.dot_general(qb, k.astype(BF16), (((1,), (1,)), ((), ())),
                        preferred_element_type=F32) * din_scr[...]
    inner = jnp.dot(s.astype(BF16), vb, preferred_element_type=F32)
    s_old = s_scr[...]
    cross = jnp.dot(qb, s_old.astype(BF16), preferred_element_type=F32) * qdec_scr[...]
    kd = (k * kdec_scr[...]).astype(BF16)
    s_new = gl_ref[h] * s_old + lax.dot_general(kd, vb, (((0,), (0,)), ((), ())),
                                                preferred_element_type=F32)
    s_scr[...] = s_new
    o = inner + cross
    xc = o - jnp.mean(o, axis=-1, keepdims=True)
    on = xc * lax.rsqrt(jnp.mean(xc * xc, axis=-1, keepdims=True) + EPS) * gn_ref[...]
    y_ref[...] = (jax.nn.silu(tile(gate_ref)) * on).astype(y_ref.dtype)

    @pl.when(c == pl.num_programs(2) - 1)
    def _():
        sout_ref[0, 0] = s_new


def retention(proj, cos, sin, gn_g, s0, *, B, T, H, hd):
    L = _pick(T, 256)
    nc = T // L
    lg = jnp.log1p(-jnp.exp2(-5.0 - jnp.arange(H, dtype=F32)))
    gl = jnp.exp(L * lg)
    has_s0 = s0 is not None
    head_major = proj.ndim == 4

    def col(sec):
        if head_major:
            return pl.BlockSpec((1, 1, L, hd), lambda b, h, c: (b, sec * H + h, c, 0))
        return pl.BlockSpec((L, hd), lambda b, h, c: (b * nc + c, sec * H + h))

    smem = pl.BlockSpec(memory_space=pltpu.SMEM)
    in_specs = [smem, smem, col(0), col(1), col(2), col(3),
                pl.BlockSpec((L, hd // 2), lambda b, h, c: (c, 0)),
                pl.BlockSpec((L, hd // 2), lambda b, h, c: (c, 0)),
                pl.BlockSpec((1, hd), lambda b, h, c: (0, h))]
    args = [lg, gl, proj, proj, proj, proj, cos, sin, gn_g.reshape(1, H * hd)]
    state_spec = pl.BlockSpec((1, 1, hd, hd), lambda b, h, c: (b, h, 0, 0))
    if has_s0:
        in_specs.append(state_spec)
        args.append(s0)
    return pl.pallas_call(
        functools.partial(_ret_kernel, L=L, hd=hd, has_s0=has_s0, head_major=head_major),
        out_shape=(jax.ShapeDtypeStruct((B * T, H * hd), BF16),
                   jax.ShapeDtypeStruct((B, H, hd, hd), F32)),
        grid=(B, H, nc),
        in_specs=in_specs,
        out_specs=(pl.BlockSpec((L, hd), lambda b, h, c: (b * nc + c, h)), state_spec),
        scratch_shapes=[pltpu.VMEM((hd, hd), F32), pltpu.VMEM((L, L), F32),
                        pltpu.VMEM((L, hd), F32), pltpu.VMEM((L, hd), F32)],
        compiler_params=_cparams(("parallel", "parallel", "arbitrary")),
        name="retention",
    )(*args)


def _cmlp_kernel(u_ref, vb_ref, lng_ref, lnb_ref, ws_ref, bst_ref, y_ref, *zv_out, G, gd, L):
    zu = jax.nn.gelu(u_ref[...])
    gv = jax.nn.gelu(vb_ref[...])
    xc = gv - jnp.mean(gv, axis=-1, keepdims=True)
    zv = xc * lax.rsqrt(jnp.mean(xc * xc, axis=-1, keepdims=True) + EPS) * lng_ref[...] + lnb_ref[...]
    if zv_out:
        zv_out[0][...] = zv
    keep = (lax.broadcasted_iota(jnp.int32, (L, L), 0) >= lax.broadcasted_iota(jnp.int32, (L, L), 1))
    for g in range(G):
        w = jnp.where(keep, ws_ref[g, :L, :L], 0.0).astype(BF16)
        mixed = jnp.dot(w, zv[:, g * gd:(g + 1) * gd].astype(BF16), preferred_element_type=F32)
        mixed = mixed + bst_ref[:, g:g + 1]
        y_ref[:, g * gd:(g + 1) * gd] = (zu[:, g * gd:(g + 1) * gd] * mixed).astype(y_ref.dtype)


def cmlp(proj, ln_g, ln_b, ws, bs, *, B, T, u_col, want_zv):
    G, chunk, _ = ws.shape
    W = ln_g.shape[0]
    gd = W // G
    L = min(T, chunk)
    nc = T // L
    out_shape = [jax.ShapeDtypeStruct((B * T, W), BF16)]
    out_specs = [pl.BlockSpec((L, W), lambda b, c: (b * nc + c, 0))]
    if want_zv:
        out_shape.append(jax.ShapeDtypeStruct((B * T, W), F32))
        out_specs.append(pl.BlockSpec((L, W), lambda b, c: (b * nc + c, 0)))
    res = pl.pallas_call(
        functools.partial(_cmlp_kernel, G=G, gd=gd, L=L),
        out_shape=tuple(out_shape),
        grid=(B, nc),
        in_specs=[pl.BlockSpec((L, W), lambda b, c: (b * nc + c, u_col)),
                  pl.BlockSpec((L, W), lambda b, c: (b * nc + c, u_col + 1)),
                  pl.BlockSpec((1, W), lambda b, c: (0, 0)),
                  pl.BlockSpec((1, W), lambda b, c: (0, 0)),
                  pl.BlockSpec((G, chunk, chunk), lambda b, c: (0, 0, 0)),
                  pl.BlockSpec((L, G), lambda b, c: (0, 0))],
        out_specs=tuple(out_specs),
        compiler_params=_cparams(("parallel", "parallel")),
        name="cmlp",
    )(proj, proj, ln_g.reshape(1, W), ln_b.reshape(1, W), ws, bs[:, :L].T)
    return res if want_zv else (res[0], None)


def _mem_kernel(x_ref, g_ref, wq_ref, qn_ref, mk_ref, mv_ref, wo_ref, o_ref, *, heads, hd):
    x = x_ref[...]
    h = _rms(x, g_ref[...]).astype(BF16)
    q = jnp.dot(h, wq_ref[...], preferred_element_type=F32)
    mk = mk_ref[0]
    mv = mv_ref[0]
    outs = []
    for hh in range(heads):
        sl = slice(hh * hd, (hh + 1) * hd)
        qh = _rms(q[:, sl], qn_ref[...]).astype(BF16)
        s = lax.dot_general(qh, mk[:, sl].astype(BF16), (((1,), (1,)), ((), ())),
                            preferred_element_type=F32) * (hd ** -0.5)
        e = jnp.exp(s - jnp.max(s, axis=-1, keepdims=True))
        p = (e / jnp.sum(e, axis=-1, keepdims=True)).astype(BF16)
        outs.append(jnp.dot(p, mv[:, sl].astype(BF16), preferred_element_type=F32))
    o = jnp.concatenate(outs, axis=-1).astype(BF16)
    o_ref[...] = x + jnp.dot(o, wo_ref[...], preferred_element_type=F32)


def mem_attn(x, g, wq, qn_g, mk, mv, wo, *, B, T, heads):
    n, d = x.shape
    mw = wq.shape[1]
    hd = mw // heads
    mlen = mk.shape[1]
    tm = _pick(T, 256)
    tpb = T // tm
    return pl.pallas_call(
        functools.partial(_mem_kernel, heads=heads, hd=hd),
        out_shape=jax.ShapeDtypeStruct((n, d), F32),
        grid=(n // tm,),
        in_specs=[pl.BlockSpec((tm, d), lambda i: (i, 0)),
                  pl.BlockSpec((1, d), lambda i: (0, 0)),
                  pl.BlockSpec((d, mw), lambda i: (0, 0)),
                  pl.BlockSpec((1, hd), lambda i: (0, 0)),
                  pl.BlockSpec((1, mlen, mw), lambda i: (i // tpb, 0, 0)),
                  pl.BlockSpec((1, mlen, mw), lambda i: (i // tpb, 0, 0)),
                  pl.BlockSpec((mw, d), lambda i: (0, 0))],
        out_specs=pl.BlockSpec((tm, d), lambda i: (i, 0)),
        compiler_params=_cparams(("parallel",)),
        name="mem_attn",
    )(x, g.reshape(1, d), wq, qn_g.reshape(1, hd), mk, mv, wo)


def _router_kernel(x_ref, g_ref, w_ref, b_ref, ids_ref, gates_ref, *, G, E):
    h = _rms(x_ref[...], g_ref[...])
    h1 = h.astype(BF16)
    h2 = (h - h1.astype(F32)).astype(BF16)
    a = jnp.dot(h1, w_ref[...], preferred_element_type=F32)
    logits = (a[:, :LANES] + a[:, LANES:] + jnp.dot(h2, w_ref[:, :LANES], preferred_element_type=F32)
              + b_ref[...])
    lane = lax.broadcasted_iota(jnp.int32, logits.shape, 1)
    neg = -jnp.inf
    is_g = lane < G
    lgm = jnp.where(is_g, logits, neg)
    gmax = jnp.max(lgm, axis=-1, keepdims=True)
    gsel = jnp.min(jnp.where(lgm == gmax, lane, LANES), axis=-1, keepdims=True)
    p_grp = 1.0 / jnp.sum(jnp.where(is_g, jnp.exp(logits - gmax), 0.0), axis=-1, keepdims=True)
    lo = G + gsel * E
    lem = jnp.where((lane >= lo) & (lane < lo + E), logits, neg)
    v1 = jnp.max(lem, axis=-1, keepdims=True)
    i1 = jnp.min(jnp.where(lem == v1, lane, LANES), axis=-1, keepdims=True)
    lem2 = jnp.where(lane == i1, neg, lem)
    v2 = jnp.max(lem2, axis=-1, keepdims=True)
    i2 = jnp.min(jnp.where(lem2 == v2, lane, LANES), axis=-1, keepdims=True)
    e2 = jnp.exp(v2 - v1)
    den = 1.0 + e2
    ids_ref[...] = jnp.where(lane == 0, i1 - G, jnp.where(lane == 1, i2 - G, 0))
    gates_ref[...] = jnp.where(lane == 0, p_grp / den, jnp.where(lane == 1, p_grp * e2 / den, 0.0))


def moe_router(x, g, w_rg, b_rg, w_re, b_re):
    n, d = x.shape
    G = w_rg.shape[1]
    E = w_re.shape[1] // G
    assert G + G * E <= LANES
    pad = LANES - G - G * E
    w = jnp.concatenate([w_rg, w_re, jnp.zeros((d, pad), F32)], axis=1)
    w1 = w.astype(BF16)
    w = jnp.concatenate([w1, (w - w1.astype(F32)).astype(BF16)], axis=1)
    b = jnp.concatenate([b_rg, b_re, jnp.zeros((pad,), F32)]).reshape(1, LANES)
    tm = _pick(n, 256)
    row = pl.BlockSpec((tm, LANES), lambda i: (i, 0))
    return pl.pallas_call(
        functools.partial(_router_kernel, G=G, E=E),
        out_shape=(jax.ShapeDtypeStruct((n, LANES), jnp.int32),
                   jax.ShapeDtypeStruct((n, LANES), F32)),
        grid=(n // tm,),
        in_specs=[pl.BlockSpec((tm, d), lambda i: (i, 0)),
                  pl.BlockSpec((1, d), lambda i: (0, 0)),
                  pl.BlockSpec((d, 2 * LANES), lambda i: (0, 0)),
                  pl.BlockSpec((1, LANES), lambda i: (0, 0))],
        out_specs=(row, row),
        compiler_params=_cparams(("parallel",)),
        name="moe_router",
    )(x, g.reshape(1, d), w, b)


def moe_dispatch(ids, n_experts, tm):
    n = ids.shape[0]
    flat = ids[:, :2].reshape(-1)
    experts = jnp.arange(n_experts, dtype=jnp.int32)
    hit = flat[:, None] == experts[None, :]
    counts = jnp.sum(hit.astype(jnp.int32), axis=0)
    padded = ((counts + tm - 1) // tm) * tm
    ends = jnp.cumsum(padded)
    starts = ends - padded
    raw_starts = jnp.cumsum(counts) - counts
    order = jnp.argsort(flat, stable=True).astype(jnp.int32)
    sorted_pos = jnp.argsort(order).astype(jnp.int32)
    pos = sorted_pos + jnp.sum(jnp.where(hit, (starts - raw_starts)[None, :], 0), axis=1)
    max_tiles = (2 * n) // tm + n_experts + FFN_GATHER_SLOTS - 1
    tile_start = jnp.arange(max_tiles, dtype=jnp.int32) * tm
    tile_expert = jnp.minimum(jnp.sum((ends[None, :] <= tile_start[:, None]).astype(jnp.int32), axis=1),
                              n_experts - 1)
    tile_src = jnp.minimum(raw_starts[tile_expert] + tile_start - starts[tile_expert], 2 * n)
    order = jnp.concatenate([order, jnp.arange(tm, dtype=jnp.int32) * 2])
    n_tiles = (ends[-1] // tm).astype(jnp.int32).reshape(1)
    return order, pos.astype(jnp.int32), tile_expert, tile_src.astype(jnp.int32), n_tiles


def _row_copy(src_hbm, dst, src_row, r, sem):
    return pltpu.make_async_copy(src_hbm.at[pl.ds(src_row, 1)], dst.at[pl.ds(r, 1)], sem)


def _row_gather_start(src_hbm, dst, idx_ref, base, rows, sem, shift=0):
    for r in range(rows):
        _row_copy(src_hbm, dst, idx_ref[base + r] >> shift, r, sem).start()


def _row_gather_wait(src_hbm, dst, rows, sem):
    for r in range(rows):
        _row_copy(src_hbm, dst, 0, r, sem).wait()


def _ffn_kernel(te_ref, ts_ref, order_ref, nt_ref, x_hbm, g_ref, wg_ref, wu_ref, wd_ref, y_ref,
                xbuf, hbuf, sem, *, tm):
    del te_ref
    i = pl.program_id(0)
    nt = nt_ref[0]
    slot = i % FFN_GATHER_SLOTS
    ahead = FFN_GATHER_SLOTS - 1

    def request(tile):
        s = tile % FFN_GATHER_SLOTS
        _row_gather_start(x_hbm, xbuf.at[s], order_ref, ts_ref[tile], tm, sem.at[s], shift=1)

    @pl.when(i == 0)
    def _():
        for t in range(ahead):
            request(t)

    @pl.when(i < nt)
    def _():
        _row_gather_wait(x_hbm, xbuf.at[slot], tm, sem.at[slot])
        hbuf[...] = _rms(xbuf[slot], g_ref[...]).astype(BF16)
        request(i + ahead)
        h = hbuf[...]
        a = (jax.nn.silu(jnp.dot(h, wg_ref[0], preferred_element_type=F32))
             * jnp.dot(h, wu_ref[0], preferred_element_type=F32))
        y_ref[...] = jnp.dot(a.astype(BF16), wd_ref[0], preferred_element_type=F32)

    @pl.when((i >= nt) & (i < nt + ahead))
    def _():
        _row_gather_wait(x_hbm, xbuf.at[slot], tm, sem.at[slot])

    @pl.when(i >= nt)
    def _():
        y_ref[...] = jnp.zeros_like(y_ref)


def moe_ffn(x, g, wg, wu, wd, order, tile_expert, tile_src, n_tiles, *, tm, first_expert):
    n, d = x.shape
    f = wg.shape[2]
    max_tiles = tile_expert.shape[0]
    grid_spec = pltpu.PrefetchScalarGridSpec(
        num_scalar_prefetch=4,
        grid=(max_tiles,),
        in_specs=[pl.BlockSpec(memory_space=pl.ANY),
                  pl.BlockSpec((1, d), lambda i, te, *_: (0, 0)),
                  pl.BlockSpec((1, d, f), lambda i, te, *_: (first_expert + te[i], 0, 0)),
                  pl.BlockSpec((1, d, f), lambda i, te, *_: (first_expert + te[i], 0, 0)),
                  pl.BlockSpec((1, f, d), lambda i, te, *_: (first_expert + te[i], 0, 0))],
        out_specs=pl.BlockSpec((tm, d), lambda i, te, *_: (i, 0)),
        scratch_shapes=[pltpu.VMEM((FFN_GATHER_SLOTS, tm, d), F32), pltpu.VMEM((tm, d), BF16),
                        pltpu.SemaphoreType.DMA((FFN_GATHER_SLOTS,))],
    )
    return pl.pallas_call(
        functools.partial(_ffn_kernel, tm=tm),
        out_shape=jax.ShapeDtypeStruct((max_tiles * tm, d), F32),
        grid_spec=grid_spec,
        compiler_params=_cparams(("arbitrary",)),
        name="moe_ffn",
    )(tile_expert, tile_src, order, n_tiles, x, g.reshape(1, d), wg, wu, wd)


def _combine_kernel(pos_ref, x_ref, gates_ref, y_hbm, *rest, tc, with_norm):
    if with_norm:
        g_ref, o_ref, h_ref, ybuf, sem = rest
    else:
        o_ref, ybuf, sem = rest
    i = pl.program_id(0)
    nsteps = pl.num_programs(0)
    slot = i % 2

    @pl.when(i == 0)
    def _():
        _row_gather_start(y_hbm, ybuf.at[0], pos_ref, 0, 2 * tc, sem.at[0])

    @pl.when(i + 1 < nsteps)
    def _():
        _row_gather_start(y_hbm, ybuf.at[1 - slot], pos_ref, (i + 1) * 2 * tc, 2 * tc, sem.at[1 - slot])

    _row_gather_wait(y_hbm, ybuf.at[slot], 2 * tc, sem.at[slot])
    gates = gates_ref[...]
    out = (x_ref[...] + gates[:, 0:1] * ybuf[slot, pl.ds(0, tc)]
           + gates[:, 1:2] * ybuf[slot, pl.ds(tc, tc)])
    o_ref[...] = out
    if with_norm:
        h_ref[...] = _rms(out, g_ref[...]).astype(h_ref.dtype)


def moe_combine(x, gates, y_rows, pos, next_g, *, tc):
    n, d = x.shape
    nt = n // tc
    with_norm = next_g is not None
    pos_tiled = pos.reshape(nt, tc, 2).transpose(0, 2, 1).reshape(-1)
    row = pl.BlockSpec((tc, d), lambda i, p: (i, 0))
    in_specs = [row, pl.BlockSpec((tc, LANES), lambda i, p: (i, 0)), pl.BlockSpec(memory_space=pl.ANY)]
    args = [pos_tiled, x, gates, y_rows]
    out_shape = jax.ShapeDtypeStruct((n, d), F32)
    out_specs = row
    if with_norm:
        in_specs.append(pl.BlockSpec((1, d), lambda i, p: (0, 0)))
        args.append(next_g.reshape(1, d))
        out_shape = (out_shape, jax.ShapeDtypeStruct((n, d), BF16))
        out_specs = (row, row)
    grid_spec = pltpu.PrefetchScalarGridSpec(
        num_scalar_prefetch=1,
        grid=(nt,),
        in_specs=in_specs,
        out_specs=out_specs,
        scratch_shapes=[pltpu.VMEM((2, 2 * tc, d), F32), pltpu.SemaphoreType.DMA((2,))],
    )
    res = pl.pallas_call(
        functools.partial(_combine_kernel, tc=tc, with_norm=with_norm),
        out_shape=out_shape,
        grid_spec=grid_spec,
        compiler_params=_cparams(("arbitrary",)),
        name="moe_combine",
    )(*args)
    return res if with_norm else (res, None)


def hier_moe(x, g, w_rg, b_rg, w_re, b_re, wg, wu, wd, *, first_expert, n_experts, next_g):
    n = x.shape[0]
    mean_load = max(1, 2 * n // n_experts)
    tm = min(256, max(32, 1 << (mean_load - 1).bit_length()))
    assert tm <= n
    ids, gates = moe_router(x, g, w_rg, b_rg, w_re, b_re)
    order, pos, tile_expert, tile_src, n_tiles = moe_dispatch(ids, n_experts, tm)
    y_rows = moe_ffn(x, g, wg, wu, wd, order, tile_expert, tile_src, n_tiles, tm=tm,
                     first_expert=first_expert)
    return moe_combine(x, gates, y_rows, pos, next_g, tc=_pick(n, 128))


def _cumsum_kernel(x_ref, c_ref, ct_ref, carry, *, L):
    @pl.when(pl.program_id(1) == 0)
    def _():
        carry[...] = jnp.zeros_like(carry)

    tri = (lax.broadcasted_iota(jnp.int32, (L, L), 0)
           >= lax.broadcasted_iota(jnp.int32, (L, L), 1)).astype(F32)
    c = jnp.dot(tri, x_ref[0], preferred_element_type=F32,
                precision=lax.Precision.HIGHEST) + carry[...]
    c_ref[0] = c * LOG2E
    ct_ref[0] = (c * LOG2E).T
    carry[...] = c[L - 1:L, :]


def cumsum_time(x):
    B, T, w = x.shape
    L = _pick(T, 256)
    return pl.pallas_call(
        functools.partial(_cumsum_kernel, L=L),
        out_shape=(jax.ShapeDtypeStruct((B, T, w), F32), jax.ShapeDtypeStruct((B, w, T), F32)),
        grid=(B, T // L),
        in_specs=[pl.BlockSpec((1, L, w), lambda b, t: (b, t, 0))],
        out_specs=(pl.BlockSpec((1, L, w), lambda b, t: (b, t, 0)),
                   pl.BlockSpec((1, w, L), lambda b, t: (b, 0, t))),
        scratch_shapes=[pltpu.VMEM((1, w), F32)],
        compiler_params=_cparams(("parallel", "arbitrary")),
        name="cumsum_time",
    )(x)


def _row_bcast(r, width):
    if width % LANES == 0:
        return jnp.tile(r, (1, width // LANES))
    return r[:, :1]


def _softmax_stats(t, cq, m_old, l_old):
    m_new = jnp.maximum(m_old, jnp.max(t, axis=-1, keepdims=True) + cq)
    alpha = jnp.exp2(m_old - m_new)
    p = jnp.exp2(t + _row_bcast(cq - m_new, t.shape[1]))
    l_new = alpha * l_old + jnp.sum(p, axis=-1, keepdims=True)
    return m_new, l_new, alpha, p


def _fox_prompt_kernel(q_ref, k_ref, v_ref, cq_ref, ck_ref, o_ref, m_scr, l_scr, acc_scr, cq_scr,
                       *, HG, tq, tk, hd):
    hg = pl.program_id(1)
    qi = pl.program_id(2)
    ki = pl.program_id(3)

    @pl.when(ki == 0)
    def _():
        m_scr[...] = jnp.full_like(m_scr, -jnp.inf)
        l_scr[...] = jnp.zeros_like(l_scr)
        acc_scr[...] = jnp.zeros_like(acc_scr)
        cq_all = cq_ref[...]
        lane = lax.broadcasted_iota(jnp.int32, cq_all.shape, 1)
        for hh in range(HG):
            col = jnp.sum(jnp.where(lane == hg * HG + hh, cq_all, 0.0), axis=-1, keepdims=True)
            cq_scr[hh] = jnp.broadcast_to(col, cq_all.shape)

    def run(masked):
        if masked:
            keep = (lax.broadcasted_iota(jnp.int32, (tq, tk), 0) + qi * tq
                    >= lax.broadcasted_iota(jnp.int32, (tq, tk), 1) + ki * tk)

        def head(hh, carry):
            s = lax.dot_general(q_ref[0, hh], k_ref[0, hh], (((1,), (1,)), ((), ())),
                                preferred_element_type=F32)
            t = s - ck_ref[0, pl.ds(hh, 1), :]
            if masked:
                t = jnp.where(keep, t, -jnp.inf)
            m_new, l_new, alpha, p = _softmax_stats(t, cq_scr[hh], m_scr[hh], l_scr[hh])
            m_scr[hh] = m_new
            l_scr[hh] = l_new
            acc_scr[hh] = alpha * acc_scr[hh] + jnp.dot(p.astype(BF16), v_ref[0, hh],
                                                        preferred_element_type=F32)
            return carry

        lax.fori_loop(0, HG, head, 0, unroll=8)

    @pl.when(ki < qi)
    def _():
        run(False)

    @pl.when(ki == qi)
    def _():
        run(True)
        for hh in range(HG):
            o_ref[:, hh * hd:(hh + 1) * hd] = (acc_scr[hh] / l_scr[hh]).astype(o_ref.dtype)


def fox_prompt(q_hm, k_hm, v_hm, c_tok, c_t, *, B, T, H, hd):
    HG = _pick(H, 16)
    tq = tk = _pick(T, 512)
    nq = T // tq
    kv_spec = pl.BlockSpec((1, HG, tk, hd), lambda b, g, qi, ki: (b, g, jnp.minimum(ki, qi), 0))
    return pl.pallas_call(
        functools.partial(_fox_prompt_kernel, HG=HG, tq=tq, tk=tk, hd=hd),
        out_shape=jax.ShapeDtypeStruct((B * T, H * hd), BF16),
        grid=(B, H // HG, nq, nq),
        in_specs=[pl.BlockSpec((1, HG, tq, hd), lambda b, g, qi, ki: (b, g, qi, 0)),
                  kv_spec, kv_spec,
                  pl.BlockSpec((tq, LANES), lambda b, g, qi, ki: (b * nq + qi, 0)),
                  pl.BlockSpec((1, HG, tk), lambda b, g, qi, ki: (b, g, jnp.minimum(ki, qi)))],
        out_specs=pl.BlockSpec((tq, HG * hd), lambda b, g, qi, ki: (b * nq + qi, g)),
        scratch_shapes=[pltpu.VMEM((HG, tq, LANES), F32), pltpu.VMEM((HG, tq, LANES), F32),
                        pltpu.VMEM((HG, tq, hd), F32), pltpu.VMEM((HG, tq, LANES), F32)],
        compiler_params=_cparams(("parallel", "parallel", "parallel", "arbitrary")),
        name="fox_prompt",
    )(q_hm, k_hm, v_hm, c_tok, c_t)


def _fox_sample_kernel(q_ref, kn_ref, vn_ref, k_hbm, v_hbm, cq_ref, cn_ref, cp_ref, o_ref,
                       cq_scr, m_scr, l_scr, acc_scr, s_scr, p_scr, kbuf, vbuf, sem, *, H, hd, Ts, tk, layer):
    b = pl.program_id(0)
    j = pl.program_id(1)

    def tile_copies(tile, slot):
        span = pl.ds(tile * tk, tk)
        for hh in range(H):
            yield pltpu.make_async_copy(k_hbm.at[layer, b, span, hh, :], kbuf.at[slot, hh], sem.at[slot, 0])
            yield pltpu.make_async_copy(v_hbm.at[layer, b, span, hh, :], vbuf.at[slot, hh], sem.at[slot, 1])

    def rows(hh):
        return slice(hh * Ts, (hh + 1) * Ts)

    def cols(hh):
        return slice(hh * hd, (hh + 1) * hd)

    def update(width, key, value, ck, keep):
        for hh in range(H):
            s = lax.dot_general(q_ref[:, cols(hh)], key(hh), (((1,), (1,)), ((), ())),
                                preferred_element_type=F32)
            t = s - ck(hh)
            s_scr[rows(hh), :width] = t if keep is None else jnp.where(keep, t, -jnp.inf)
        m_new, l_new, alpha, p = _softmax_stats(s_scr[:, :width], cq_scr[...], m_scr[...], l_scr[...])
        m_scr[...] = m_new
        l_scr[...] = l_new
        p_scr[:, :width] = p.astype(BF16)
        for hh in range(H):
            acc_scr[rows(hh)] = alpha[rows(hh)] * acc_scr[rows(hh)] + jnp.dot(
                p_scr[rows(hh), :width], value(hh), preferred_element_type=F32)

    @pl.when(j == 0)
    def _():
        for cp in tile_copies(0, 0):
            cp.start()
        m_scr[...] = jnp.full_like(m_scr, -jnp.inf)
        l_scr[...] = jnp.zeros_like(l_scr)
        acc_scr[...] = jnp.zeros_like(acc_scr)
        for hh in range(H):
            cq_scr[rows(hh)] = jnp.broadcast_to(cq_ref[0, :, hh:hh + 1], (Ts, LANES))
        keep = (lax.broadcasted_iota(jnp.int32, (Ts, Ts), 0) >= lax.broadcasted_iota(jnp.int32, (Ts, Ts), 1))
        update(Ts, lambda hh: kn_ref[:, cols(hh)].astype(BF16), lambda hh: vn_ref[:, cols(hh)].astype(BF16),
               lambda hh: cn_ref[0, hh:hh + 1, :Ts], keep)

    @pl.when(j > 0)
    def _():
        tile = j - 1
        slot = tile % 2
        for cp in tile_copies(tile, slot):
            cp.wait()

        @pl.when(j < pl.num_programs(1) - 1)
        def _():
            for cp in tile_copies(tile + 1, 1 - slot):
                cp.start()

        update(tk, lambda hh: kbuf[slot, hh].astype(BF16), lambda hh: vbuf[slot, hh].astype(BF16),
               lambda hh: cp_ref[0, hh:hh + 1, :], None)

    @pl.when(j == pl.num_programs(1) - 1)
    def _():
        for hh in range(H):
            o_ref[:, cols(hh)] = (acc_scr[rows(hh)] / l_scr[rows(hh)]).astype(o_ref.dtype)


def fox_sample(q, k_new, v_new, k_past, v_past, c_tok, c_t, *, B, Ts, P, H, hd, layer):
    w = H * hd
    tk = _pick(P, 256)
    npk = P // tk
    assert P % LANES == 0 and Ts <= LANES and P % Ts == 0 and hd == LANES
    past = pl.BlockSpec(memory_space=pl.ANY)
    new = pl.BlockSpec((Ts, w), lambda b, j: (b, 0))
    return pl.pallas_call(
        functools.partial(_fox_sample_kernel, H=H, hd=hd, Ts=Ts, tk=tk, layer=layer),
        out_shape=jax.ShapeDtypeStruct((B * Ts, w), BF16),
        grid=(B, npk + 1),
        in_specs=[new, new, new, past, past,
                  pl.BlockSpec((1, Ts, LANES), lambda b, j: (b, P // Ts, 0)),
                  pl.BlockSpec((1, LANES, LANES), lambda b, j: (b, 0, P // LANES)),
                  pl.BlockSpec((1, LANES, tk), lambda b, j: (b, 0, jnp.maximum(j - 1, 0)))],
        out_specs=new,
        scratch_shapes=[pltpu.VMEM((H * Ts, LANES), F32), pltpu.VMEM((H * Ts, LANES), F32),
                        pltpu.VMEM((H * Ts, LANES), F32), pltpu.VMEM((H * Ts, hd), F32),
                        pltpu.VMEM((H * Ts, tk), F32), pltpu.VMEM((H * Ts, tk), BF16),
                        pltpu.VMEM((2, H, tk, hd), F32), pltpu.VMEM((2, H, tk, hd), F32),
                        pltpu.SemaphoreType.DMA((2, 2))],
        compiler_params=_cparams(("arbitrary", "arbitrary")),
        name="fox_sample",
    )(q, k_new, v_new, k_past, v_past, c_tok, c_t, c_t)


def fox_project(h, w_in, layer, wf, b_f, qn_g, kn_g, *, B, T, H, hd, head_major):
    m, _ = h.shape
    w = H * hd
    tm, tn = (_pick(T, 1024), _pick(w, 512)) if head_major else _mm_tiles(m, w)
    tpb = T // tm if head_major else 1
    tok = pl.BlockSpec((tm, tn), lambda i, j: (i, j))
    hm = pl.BlockSpec((1, tn // hd, tm, hd), lambda i, j: (i // tpb, j, i % tpb, 0))
    gain = pl.BlockSpec((1, hd), lambda i, j: (0, 0))
    tok_f32 = jax.ShapeDtypeStruct((m, w), F32)
    tok_bf16 = jax.ShapeDtypeStruct((m, w), BF16)
    hm_bf16 = jax.ShapeDtypeStruct((B, H, T, hd), BF16)

    def proj(section, g, norm, scale, shapes, specs, kinds, name):
        ep = functools.partial(_ep_heads, hd=hd, norm=norm, scale=scale, kinds=kinds)
        extras = (g.reshape(1, hd),) if norm else ()
        especs = (gain,) if norm else ()
        return matmul(h, w_in, layer=layer, col0=section * w, n=w, epilogue=ep, extras=extras,
                      extra_specs=especs, out_shapes=shapes, out_specs=specs, tm=tm, tn=tn, name=name)

    q_scale = hd ** -0.5 * LOG2E
    if head_major:
        (q,) = proj(0, qn_g, True, q_scale, (hm_bf16,), (hm,), ("head",), "fox_q")
        k, k_hm = proj(1, kn_g, True, 1.0, (tok_f32, hm_bf16), (tok, hm), ("tok", "head"), "fox_k")
        v, v_hm = proj(2, None, False, 1.0, (tok_f32, hm_bf16), (tok, hm), ("tok", "head"), "fox_v")
    else:
        (q,) = proj(0, qn_g, True, q_scale, (tok_bf16,), (tok,), ("tok",), "fox_q")
        (k,) = proj(1, kn_g, True, 1.0, (tok_f32,), (tok,), ("tok",), "fox_k")
        (v,) = proj(2, None, False, 1.0, (tok_f32,), (tok,), ("tok",), "fox_v")
        k_hm = v_hm = None
    tmf = _pick(m, 1024)
    logf = matmul(h, wf, layer=0, n=LANES, epilogue=_ep_logsig, extras=(b_f,),
                  extra_specs=(pl.BlockSpec((1, LANES), lambda i, j: (0, 0)),),
                  out_shapes=jax.ShapeDtypeStruct((m, LANES), F32),
                  out_specs=pl.BlockSpec((tmf, LANES), lambda i, j: (i, 0)),
                  tm=tmf, tn=LANES, name="fox_logf")
    return q, k, v, logf, k_hm, v_hm


def _stack(parts):
    return parts[0][None] if len(parts) == 1 else jnp.stack(parts)


def _rope_tables(pos, hd):
    half = hd // 2
    inv = ROPE_BASE ** (-jnp.arange(half, dtype=F32) / half)
    ang = pos.astype(F32)[:, None] * inv[None, :]
    return jnp.cos(ang), jnp.sin(ang)


def kernel(x_prompt, x_sample, mem_prompt, state_ret, cache_fox_k, cache_fox_v, cache_fox_logf, cache_mem_k, cache_mem_v, norm_mix, norm_mem, norm_moe, even_w_in, ret_gn_g, cmlp_ln_g, cmlp_ln_b, cmlp_ws, cmlp_bs, even_w_out, odd_w_in, fox_b_f, fox_qn_g, fox_kn_g, odd_w_out, mem_src_g, mem_w_q, mem_w_k, mem_w_v, mem_qn_g, mem_kn_g, mem_w_o, moe_w_rg, moe_b_rg, moe_w_re, moe_b_re, moe_w_gate, moe_w_up, moe_w_down):
    Bp, Tp, D = x_prompt.shape
    Bs, Ts, _ = x_sample.shape
    depth = norm_mix.shape[0]
    P = cache_fox_k.shape[2]
    RH, RHD = state_ret.shape[2], state_ret.shape[3]
    RW = RH * RHD
    CW = cmlp_ln_g.shape[1]
    FH, FHD = cache_fox_k.shape[3], cache_fox_k.shape[4]
    FW = FH * FHD
    MH, MHD = cache_mem_k.shape[3], cache_mem_k.shape[4]
    MLEN = mem_prompt.shape[1]
    NG, NE, _, DF = moe_w_gate.shape[1:]
    assert (4 * RW) % CW == 0 and FHD == LANES and FH <= LANES

    xp = x_prompt.reshape(Bp * Tp, D)
    xs = x_sample.reshape(Bs * Ts, D)
    mem2d = mem_prompt.reshape(Bp * MLEN, D)
    cos_p, sin_p = _rope_tables(jnp.arange(Tp), RHD)
    cos_s, sin_s = _rope_tables(P + jnp.arange(Ts), RHD)

    wg_all = moe_w_gate.reshape(depth * NG * NE, D, DF).astype(BF16)
    wu_all = moe_w_up.reshape(depth * NG * NE, D, DF).astype(BF16)
    wd_all = moe_w_down.reshape(depth * NG * NE, DF, D).astype(BF16)

    ret_S_p, ret_S_s, cmlp_v_s = [], [], []
    fk_p, fv_p, fl_p, fk_s, fv_s, fl_s = [], [], [], [], [], []
    mk_p, mv_p = [], []
    for i in range(depth):
        j = i // 2
        if i == 0:
            hp = rmsnorm_rows(xp, norm_mix[i])
            hs = rmsnorm_rows(xs, norm_mix[i])
        if i % 2 == 0:
            def even(h, x, cos, sin, s0, B, T, want_zv, head_major):
                if head_major:
                    tm, tn = _pick(T, 1024), _pick(4 * RW, 512)
                    tpb = T // tm
                    (proj_ret,) = matmul(
                        h, even_w_in, layer=j, n=4 * RW,
                        epilogue=functools.partial(_ep_heads, hd=RHD, norm=False, scale=1.0, kinds=("head",)),
                        out_shapes=(jax.ShapeDtypeStruct((B, 4 * RH, T, RHD), F32),),
                        out_specs=(pl.BlockSpec((1, tn // RHD, tm, RHD),
                                                lambda a, b: (a // tpb, b, a % tpb, 0)),),
                        tm=tm, tn=tn, name="even_in_ret")
                else:
                    proj_ret = matmul_plain(h, even_w_in, F32, "even_in_ret", layer=j, n=4 * RW)
                proj_mlp = matmul_plain(h, even_w_in, F32, "even_in_mlp", layer=j, col0=4 * RW)
                y_a, S = retention(proj_ret, cos, sin, ret_gn_g[j], s0, B=B, T=T, H=RH, hd=RHD)
                y_b, zv = cmlp(proj_mlp, cmlp_ln_g[j], cmlp_ln_b[j], cmlp_ws[j], cmlp_bs[j],
                               B=B, T=T, u_col=0, want_zv=want_zv)
                return matmul2_residual(y_a, y_b, even_w_out, x, "even_out", layer=j), S, zv

            xp, Sp, _ = even(hp, xp, cos_p, sin_p, None, Bp, Tp, False, True)
            xs, Ss, zvs = even(hs, xs, cos_s, sin_s, state_ret[j], Bs, Ts, True, False)
            ret_S_p.append(Sp)
            ret_S_s.append(Ss)
            cmlp_v_s.append(zvs.reshape(Bs, Ts, CW))
        else:
            wf = jnp.pad(odd_w_in[j, :, 3 * FW:], ((0, 0), (0, LANES - FH)))[None]
            b_f = jnp.pad(fox_b_f[j], (0, LANES - FH)).reshape(1, LANES)
            q, k, v, logf, k_hm, v_hm = fox_project(hp, odd_w_in, j, wf, b_f, fox_qn_g[j], fox_kn_g[j],
                                                    B=Bp, T=Tp, H=FH, hd=FHD, head_major=True)
            c_tok, c_t = cumsum_time(logf.reshape(Bp, Tp, LANES))
            o = fox_prompt(q, k_hm, v_hm, c_tok.reshape(Bp * Tp, LANES), c_t[:, :FH], B=Bp, T=Tp, H=FH, hd=FHD)
            xp = matmul_plain(o, odd_w_out, F32, "odd_out", layer=j, residual=xp)
            fk_p.append(k.reshape(Bp, Tp, FH, FHD))
            fv_p.append(v.reshape(Bp, Tp, FH, FHD))
            fl_p.append(logf[:, :FH].reshape(Bp, Tp, FH))
            q, k, v, logf, _, _ = fox_project(hs, odd_w_in, j, wf, b_f, fox_qn_g[j], fox_kn_g[j],
                                              B=Bs, T=Ts, H=FH, hd=FHD, head_major=False)
            t_pad = -(-(P + Ts) // 256) * 256
            seq = jnp.concatenate([jnp.pad(cache_fox_logf[j], ((0, 0), (0, 0), (0, LANES - FH))),
                                   logf.reshape(Bs, Ts, LANES),
                                   jnp.zeros((Bs, t_pad - P - Ts, LANES), F32)], axis=1)
            c_tok, c_t = cumsum_time(seq)
            o = fox_sample(q, k, v, cache_fox_k, cache_fox_v, c_tok, c_t,
                           B=Bs, Ts=Ts, P=P, H=FH, hd=FHD, layer=j)
            xs = matmul_plain(o, odd_w_out, F32, "odd_out", layer=j, residual=xs)
            fk_s.append(k.reshape(Bs, Ts, FH, FHD))
            fv_s.append(v.reshape(Bs, Ts, FH, FHD))
            fl_s.append(logf[:, :FH].reshape(Bs, Ts, FH))
        m_n = rmsnorm_rows(mem2d, mem_src_g[i])
        mw = MH * MHD
        tmm, tnm = _pick(Bp * MLEN, 1024), _pick(mw, 512)
        tile = pl.BlockSpec((tmm, tnm), lambda a, b: (a, b))
        (mk,) = matmul(m_n, mem_w_k, layer=i, n=mw,
                       epilogue=functools.partial(_ep_heads, hd=MHD, norm=True, scale=1.0, kinds=("tok",)),
                       extras=(mem_kn_g[i].reshape(1, MHD),),
                       extra_specs=(pl.BlockSpec((1, MHD), lambda a, b: (0, 0)),),
                       out_shapes=(jax.ShapeDtypeStruct((Bp * MLEN, mw), F32),), out_specs=(tile,),
                       tm=tmm, tn=tnm, name="mem_k")
        mv = matmul_plain(m_n, mem_w_v, F32, "mem_v", layer=i)
        mk_p.append(mk.reshape(Bp, MLEN, MH, MHD))
        mv_p.append(mv.reshape(Bp, MLEN, MH, MHD))
        wq_m = mem_w_q[i].astype(BF16)
        wo_m = mem_w_o[i].astype(BF16)
        xp = mem_attn(xp, norm_mem[i], wq_m, mem_qn_g[i], mk.reshape(Bp, MLEN, mw), mv.reshape(Bp, MLEN, mw),
                      wo_m, B=Bp, T=Tp, heads=MH)
        xs = mem_attn(xs, norm_mem[i], wq_m, mem_qn_g[i], cache_mem_k[i].reshape(Bs, MLEN, mw),
                      cache_mem_v[i].reshape(Bs, MLEN, mw), wo_m, B=Bs, T=Ts, heads=MH)
        moe = (norm_moe[i], moe_w_rg[i], moe_b_rg[i], moe_w_re[i], moe_b_re[i], wg_all, wu_all, wd_all)
        next_g = norm_mix[i + 1] if i + 1 < depth else None
        xp, hp = hier_moe(xp, *moe, first_expert=i * NG * NE, n_experts=NG * NE, next_g=next_g)
        xs, hs = hier_moe(xs, *moe, first_expert=i * NG * NE, n_experts=NG * NE, next_g=next_g)

    return (xp.reshape(Bp, Tp, D), xs.reshape(Bs, Ts, D),
            _stack(ret_S_p), _stack(ret_S_s), _stack(cmlp_v_s),
            _stack(fk_p), _stack(fv_p), _stack(fl_p),
            _stack(fk_s), _stack(fv_s), _stack(fl_s),
            _stack(mk_p), _stack(mv_p))
```

```python
import functools

import jax
import jax.numpy as jnp
from jax import lax
from jax.experimental import pallas as pl
from jax.experimental.pallas import tpu as pltpu

F32 = jnp.float32
BF16 = jnp.bfloat16
EPS = 1e-6
ROPE_BASE = 10000.0
LOG2E = 1.4426950408889634
LANES = 128
VMEM_LIMIT_BYTES = 56 * 1024 * 1024
FFN_GATHER_SLOTS = 3


def _cparams(semantics):
    return pltpu.CompilerParams(dimension_semantics=semantics,
                                vmem_limit_bytes=VMEM_LIMIT_BYTES)


def _pick(n, pref):
    t = min(n, pref)
    while n % t:
        t //= 2
    assert t >= 1
    return t


def _rms(x, g):
    return x * lax.rsqrt(jnp.mean(x * x, axis=-1, keepdims=True) + EPS) * g


def _rmsnorm_kernel(x_ref, g_ref, o_ref):
    o_ref[...] = _rms(x_ref[...], g_ref[...]).astype(o_ref.dtype)


def rmsnorm_rows(x, g, out_dtype=BF16):
    n, d = x.shape
    tm = _pick(n, 512)
    return pl.pallas_call(
        _rmsnorm_kernel,
        out_shape=jax.ShapeDtypeStruct((n, d), out_dtype),
        grid=(n // tm,),
        in_specs=[pl.BlockSpec((tm, d), lambda i: (i, 0)),
                  pl.BlockSpec((1, d), lambda i: (0, 0))],
        out_specs=pl.BlockSpec((tm, d), lambda i: (i, 0)),
        compiler_params=_cparams(("parallel",)),
        name="rmsnorm_rows",
    )(x, g.reshape(1, d))


def _mm_kernel(*refs, n_extra, epilogue):
    a_ref, w_ref = refs[0], refs[1]
    extra = refs[2:2 + n_extra]
    outs = refs[2 + n_extra:]
    acc = jnp.dot(a_ref[...], w_ref[...].astype(BF16), preferred_element_type=F32)
    epilogue(acc, extra, outs)


def _ep_store(acc, extra, outs):
    outs[0][...] = acc.astype(outs[0].dtype)


def _ep_residual(acc, extra, outs):
    outs[0][...] = extra[0][...] + acc


def _ep_logsig(acc, extra, outs):
    outs[0][...] = jax.nn.log_sigmoid(acc + extra[0][...])


def _ep_heads(acc, extra, outs, *, hd, norm, scale, kinds):
    tn = acc.shape[1]
    for hh in range(tn // hd):
        blk = acc[:, hh * hd:(hh + 1) * hd]
        if norm:
            blk = _rms(blk, extra[0][...])
        if scale != 1.0:
            blk = blk * scale
        for o_ref, kind in zip(outs, kinds):
            if kind == "tok":
                o_ref[:, hh * hd:(hh + 1) * hd] = blk.astype(o_ref.dtype)
            else:
                o_ref[0, hh] = blk.astype(o_ref.dtype)


def _mm_tiles(m, n):
    tm = _pick(m, 1024)
    return tm, _pick(n, 512 if m > tm or tm > 256 else 1024)


def matmul(a, w, *, layer, col0=0, n, epilogue=_ep_store, extras=(), extra_specs=(), out_shapes,
           out_specs, tm, tn, name):
    m, k = a.shape
    assert w.shape[1] == k and m % tm == 0 and n % tn == 0 and col0 % tn == 0
    kern = functools.partial(_mm_kernel, n_extra=len(extras), epilogue=epilogue)
    return pl.pallas_call(
        kern,
        out_shape=out_shapes,
        grid=(m // tm, n // tn),
        in_specs=[pl.BlockSpec((tm, k), lambda i, j: (i, 0)),
                  pl.BlockSpec((None, k, tn), lambda i, j: (layer, 0, j + col0 // tn))] + list(extra_specs),
        out_specs=out_specs,
        compiler_params=_cparams(("parallel", "parallel")),
        name=name,
    )(a, w, *extras)


def matmul_plain(a, w, out_dtype, name, *, layer, col0=0, n=None, residual=None):
    m, _ = a.shape
    n = w.shape[2] - col0 if n is None else n
    tm, tn = _mm_tiles(m, n)
    tile = pl.BlockSpec((tm, tn), lambda i, j: (i, j))
    if residual is None:
        return matmul(a, w, layer=layer, col0=col0, n=n, out_shapes=jax.ShapeDtypeStruct((m, n), out_dtype),
                      out_specs=tile, tm=tm, tn=tn, name=name)
    return matmul(a, w, layer=layer, col0=col0, n=n, epilogue=_ep_residual, extras=(residual,),
                  extra_specs=(tile,), out_shapes=jax.ShapeDtypeStruct((m, n), F32), out_specs=tile,
                  tm=tm, tn=tn, name=name)


def _mm2_residual_kernel(a1_ref, w1_ref, a2_ref, w2_ref, res_ref, o_ref):
    o_ref[...] = (res_ref[...]
                  + jnp.dot(a1_ref[...], w1_ref[...].astype(BF16), preferred_element_type=F32)
                  + jnp.dot(a2_ref[...], w2_ref[...].astype(BF16), preferred_element_type=F32))


def matmul2_residual(a1, a2, w, residual, name, *, layer):
    m, k1 = a1.shape
    k2 = a2.shape[1]
    n = w.shape[2]
    assert k1 == k2 and w.shape[1] == k1 + k2
    tm, tn = _mm_tiles(m, n)
    tile = pl.BlockSpec((tm, tn), lambda i, j: (i, j))
    return pl.pallas_call(
        _mm2_residual_kernel,
        out_shape=jax.ShapeDtypeStruct((m, n), F32),
        grid=(m // tm, n // tn),
        in_specs=[pl.BlockSpec((tm, k1), lambda i, j: (i, 0)),
                  pl.BlockSpec((None, k1, tn), lambda i, j: (layer, 0, j)),
                  pl.BlockSpec((tm, k2), lambda i, j: (i, 0)),
                  pl.BlockSpec((None, k2, tn), lambda i, j: (layer, 1, j)),
                  tile],
        out_specs=tile,
        compiler_params=_cparams(("parallel", "parallel")),
        name=name,
    )(a1, w, a2, w, residual)


def _ret_kernel(lg_ref, gl_ref, q_ref, k_ref, v_ref, gate_ref, cos_ref, sin_ref, gn_ref, *rest,
                L, hd, HB, has_s0, head_major):
    if has_s0:
        s0_ref, y_ref, sout_ref, s_scr, din_scr, qdec_scr, kdec_scr = rest
    else:
        y_ref, sout_ref, s_scr, din_scr, qdec_scr, kdec_scr = rest
    hb = pl.program_id(1)
    c = pl.program_id(2)

    @pl.when(c == 0)
    def _():
        n_col = lax.broadcasted_iota(jnp.int32, (L, 1), 0).astype(F32)
        diff = (lax.broadcasted_iota(jnp.int32, (L, L), 0)
                - lax.broadcasted_iota(jnp.int32, (L, L), 1)).astype(F32)
        for hh in range(HB):
            s_scr[hh] = s0_ref[0, hh] if has_s0 else jnp.zeros((hd, hd), F32)
            lg = lg_ref[hb * HB + hh]
            din_scr[hh] = jnp.where(diff >= 0, jnp.exp(diff * lg), 0.0)
            qdec_scr[hh] = jnp.broadcast_to(jnp.exp((n_col + 1.0) * lg), (L, hd))
            kdec_scr[hh] = jnp.broadcast_to(jnp.exp((L - 1.0 - n_col) * lg), (L, hd))

    half = hd // 2
    cos = cos_ref[...]
    sin = sin_ref[...]

    def rope(x):
        x1, x2 = x[:, :half], x[:, half:]
        return jnp.concatenate([x1 * cos - x2 * sin, x1 * sin + x2 * cos], axis=-1)

    def head(hh):
        cols = slice(hh * hd, (hh + 1) * hd)

        def tile(ref):
            return ref[0, hh] if head_major else ref[:, cols]

        q = rope(tile(q_ref))
        k = rope(tile(k_ref)) * (hd ** -0.5)
        vb = tile(v_ref).astype(BF16)
        qb = q.astype(BF16)
        s = lax.dot_general(qb, k.astype(BF16), (((1,), (1,)), ((), ())),
                            preferred_element_type=F32) * din_scr[hh]
        inner = jnp.dot(s.astype(BF16), vb, preferred_element_type=F32)
        s_old = s_scr[hh]
        cross = jnp.dot(qb, s_old.astype(BF16), preferred_element_type=F32) * qdec_scr[hh]
        kd = (k * kdec_scr[hh]).astype(BF16)
        s_new = gl_ref[hb * HB + hh] * s_old + lax.dot_general(kd, vb, (((0,), (0,)), ((), ())),
                                                               preferred_element_type=F32)
        s_scr[hh] = s_new
        o = inner + cross
        xc = o - jnp.mean(o, axis=-1, keepdims=True)
        on = xc * lax.rsqrt(jnp.mean(xc * xc, axis=-1, keepdims=True) + EPS) * gn_ref[:, cols]
        y_ref[:, cols] = (jax.nn.silu(tile(gate_ref)) * on).astype(y_ref.dtype)

        @pl.when(c == pl.num_programs(2) - 1)
        def _():
            sout_ref[0, hh] = s_new

    for hh in range(HB):
        head(hh)


def retention(proj, cos, sin, gn_g, s0, *, B, T, H, hd):
    L = _pick(T, 256)
    nc = T // L
    HB = _pick(H, 2)
    lg = jnp.log1p(-jnp.exp2(-5.0 - jnp.arange(H, dtype=F32)))
    gl = jnp.exp(L * lg)
    has_s0 = s0 is not None
    head_major = proj.ndim == 4

    def col(sec):
        first = sec * H // HB
        if head_major:
            return pl.BlockSpec((1, HB, L, hd), lambda b, h, c: (b, first + h, c, 0))
        return pl.BlockSpec((L, HB * hd), lambda b, h, c: (b * nc + c, first + h))

    smem = pl.BlockSpec(memory_space=pltpu.SMEM)
    in_specs = [smem, smem, col(0), col(1), col(2), col(3),
                pl.BlockSpec((L, hd // 2), lambda b, h, c: (c, 0)),
                pl.BlockSpec((L, hd // 2), lambda b, h, c: (c, 0)),
                pl.BlockSpec((1, HB * hd), lambda b, h, c: (0, h))]
    args = [lg, gl, proj, proj, proj, proj, cos, sin, gn_g.reshape(1, H * hd)]
    state_spec = pl.BlockSpec((1, HB, hd, hd), lambda b, h, c: (b, h, 0, 0))
    if has_s0:
        in_specs.append(state_spec)
        args.append(s0)
    return pl.pallas_call(
        functools.partial(_ret_kernel, L=L, hd=hd, HB=HB, has_s0=has_s0, head_major=head_major),
        out_shape=(jax.ShapeDtypeStruct((B * T, H * hd), BF16),
                   jax.ShapeDtypeStruct((B, H, hd, hd), F32)),
        grid=(B, H // HB, nc),
        in_specs=in_specs,
        out_specs=(pl.BlockSpec((L, HB * hd), lambda b, h, c: (b * nc + c, h)), state_spec),
        scratch_shapes=[pltpu.VMEM((HB, hd, hd), F32), pltpu.VMEM((HB, L, L), F32),
                        pltpu.VMEM((HB, L, hd), F32), pltpu.VMEM((HB, L, hd), F32)],
        compiler_params=_cparams(("parallel", "parallel", "arbitrary")),
        name="retention",
    )(*args)


def _cmlp_kernel(u_ref, vb_ref, lng_ref, lnb_ref, ws_ref, bst_ref, y_ref, *zv_out, G, gd, L):
    zu = jax.nn.gelu(u_ref[...])
    gv = jax.nn.gelu(vb_ref[...])
    xc = gv - jnp.mean(gv, axis=-1, keepdims=True)
    zv = xc * lax.rsqrt(jnp.mean(xc * xc, axis=-1, keepdims=True) + EPS) * lng_ref[...] + lnb_ref[...]
    if zv_out:
        zv_out[0][...] = zv
    keep = (lax.broadcasted_iota(jnp.int32, (L, L), 0) >= lax.broadcasted_iota(jnp.int32, (L, L), 1))
    for g in range(G):
        w = jnp.where(keep, ws_ref[g, :L, :L], 0.0).astype(BF16)
        mixed = jnp.dot(w, zv[:, g * gd:(g + 1) * gd].astype(BF16), preferred_element_type=F32)
        mixed = mixed + bst_ref[:, g:g + 1]
        y_ref[:, g * gd:(g + 1) * gd] = (zu[:, g * gd:(g + 1) * gd] * mixed).astype(y_ref.dtype)


def cmlp(proj, ln_g, ln_b, ws, bs, *, B, T, u_col, want_zv):
    G, chunk, _ = ws.shape
    W = ln_g.shape[0]
    gd = W // G
    L = min(T, chunk)
    nc = T // L
    out_shape = [jax.ShapeDtypeStruct((B * T, W), BF16)]
    out_specs = [pl.BlockSpec((L, W), lambda b, c: (b * nc + c, 0))]
    if want_zv:
        out_shape.append(jax.ShapeDtypeStruct((B * T, W), F32))
        out_specs.append(pl.BlockSpec((L, W), lambda b, c: (b * nc + c, 0)))
    res = pl.pallas_call(
        functools.partial(_cmlp_kernel, G=G, gd=gd, L=L),
        out_shape=tuple(out_shape),
        grid=(B, nc),
        in_specs=[pl.BlockSpec((L, W), lambda b, c: (b * nc + c, u_col)),
                  pl.BlockSpec((L, W), lambda b, c: (b * nc + c, u_col + 1)),
                  pl.BlockSpec((1, W), lambda b, c: (0, 0)),
                  pl.BlockSpec((1, W), lambda b, c: (0, 0)),
                  pl.BlockSpec((G, chunk, chunk), lambda b, c: (0, 0, 0)),
                  pl.BlockSpec((L, G), lambda b, c: (0, 0))],
        out_specs=tuple(out_specs),
        compiler_params=_cparams(("parallel", "parallel")),
        name="cmlp",
    )(proj, proj, ln_g.reshape(1, W), ln_b.reshape(1, W), ws, bs[:, :L].T)
    return res if want_zv else (res[0], None)


def _mem_kernel(x_ref, g_ref, wq_ref, qn_ref, mk_ref, mv_ref, wo_ref, o_ref, *, heads, hd):
    x = x_ref[...]
    h = _rms(x, g_ref[...]).astype(BF16)
    q = jnp.dot(h, wq_ref[...], preferred_element_type=F32)
    mk = mk_ref[0]
    mv = mv_ref[0]
    outs = []
    for hh in range(heads):
        sl = slice(hh * hd, (hh + 1) * hd)
        qh = _rms(q[:, sl], qn_ref[...]).astype(BF16)
        s = lax.dot_general(qh, mk[:, sl].astype(BF16), (((1,), (1,)), ((), ())),
                            preferred_element_type=F32) * (hd ** -0.5)
        e = jnp.exp(s - jnp.max(s, axis=-1, keepdims=True))
        p = (e / jnp.sum(e, axis=-1, keepdims=True)).astype(BF16)
        outs.append(jnp.dot(p, mv[:, sl].astype(BF16), preferred_element_type=F32))
    o = jnp.concatenate(outs, axis=-1).astype(BF16)
    o_ref[...] = x + jnp.dot(o, wo_ref[...], preferred_element_type=F32)


def mem_attn(x, g, wq, qn_g, mk, mv, wo, *, B, T, heads):
    n, d = x.shape
    mw = wq.shape[1]
    hd = mw // heads
    mlen = mk.shape[1]
    tm = _pick(T, 256)
    tpb = T // tm
    return pl.pallas_call(
        functools.partial(_mem_kernel, heads=heads, hd=hd),
        out_shape=jax.ShapeDtypeStruct((n, d), F32),
        grid=(n // tm,),
        in_specs=[pl.BlockSpec((tm, d), lambda i: (i, 0)),
                  pl.BlockSpec((1, d), lambda i: (0, 0)),
                  pl.BlockSpec((d, mw), lambda i: (0, 0)),
                  pl.BlockSpec((1, hd), lambda i: (0, 0)),
                  pl.BlockSpec((1, mlen, mw), lambda i: (i // tpb, 0, 0)),
                  pl.BlockSpec((1, mlen, mw), lambda i: (i // tpb, 0, 0)),
                  pl.BlockSpec((mw, d), lambda i: (0, 0))],
        out_specs=pl.BlockSpec((tm, d), lambda i: (i, 0)),
        compiler_params=_cparams(("parallel",)),
        name="mem_attn",
    )(x, g.reshape(1, d), wq, qn_g.reshape(1, hd), mk, mv, wo)


def _router_kernel(x_ref, g_ref, w_ref, b_ref, ids_ref, gates_ref, *, G, E):
    h = _rms(x_ref[...], g_ref[...])
    h1 = h.astype(BF16)
    h2 = (h - h1.astype(F32)).astype(BF16)
    a = jnp.dot(h1, w_ref[...], preferred_element_type=F32)
    logits = (a[:, :LANES] + a[:, LANES:] + jnp.dot(h2, w_ref[:, :LANES], preferred_element_type=F32)
              + b_ref[...])
    lane = lax.broadcasted_iota(jnp.int32, logits.shape, 1)
    neg = -jnp.inf
    is_g = lane < G
    lgm = jnp.where(is_g, logits, neg)
    gmax = jnp.max(lgm, axis=-1, keepdims=True)
    gsel = jnp.min(jnp.where(lgm == gmax, lane, LANES), axis=-1, keepdims=True)
    p_grp = 1.0 / jnp.sum(jnp.where(is_g, jnp.exp(logits - gmax), 0.0), axis=-1, keepdims=True)
    lo = G + gsel * E
    lem = jnp.where((lane >= lo) & (lane < lo + E), logits, neg)
    v1 = jnp.max(lem, axis=-1, keepdims=True)
    i1 = jnp.min(jnp.where(lem == v1, lane, LANES), axis=-1, keepdims=True)
    lem2 = jnp.where(lane == i1, neg, lem)
    v2 = jnp.max(lem2, axis=-1, keepdims=True)
    i2 = jnp.min(jnp.where(lem2 == v2, lane, LANES), axis=-1, keepdims=True)
    e2 = jnp.exp(v2 - v1)
    den = 1.0 + e2
    ids_ref[...] = jnp.where(lane == 0, i1 - G, jnp.where(lane == 1, i2 - G, 0))
    gates_ref[...] = jnp.where(lane == 0, p_grp / den, jnp.where(lane == 1, p_grp * e2 / den, 0.0))


def moe_router(x, g, w_rg, b_rg, w_re, b_re):
    n, d = x.shape
    G = w_rg.shape[1]
    E = w_re.shape[1] // G
    assert G + G * E <= LANES
    pad = LANES - G - G * E
    w = jnp.concatenate([w_rg, w_re, jnp.zeros((d, pad), F32)], axis=1)
    w1 = w.astype(BF16)
    w = jnp.concatenate([w1, (w - w1.astype(F32)).astype(BF16)], axis=1)
    b = jnp.concatenate([b_rg, b_re, jnp.zeros((pad,), F32)]).reshape(1, LANES)
    tm = _pick(n, 256)
    row = pl.BlockSpec((tm, LANES), lambda i: (i, 0))
    return pl.pallas_call(
        functools.partial(_router_kernel, G=G, E=E),
        out_shape=(jax.ShapeDtypeStruct((n, LANES), jnp.int32),
                   jax.ShapeDtypeStruct((n, LANES), F32)),
        grid=(n // tm,),
        in_specs=[pl.BlockSpec((tm, d), lambda i: (i, 0)),
                  pl.BlockSpec((1, d), lambda i: (0, 0)),
                  pl.BlockSpec((d, 2 * LANES), lambda i: (0, 0)),
                  pl.BlockSpec((1, LANES), lambda i: (0, 0))],
        out_specs=(row, row),
        compiler_params=_cparams(("parallel",)),
        name="moe_router",
    )(x, g.reshape(1, d), w, b)


def moe_dispatch(ids, n_experts, tm):
    n = ids.shape[0]
    flat = ids[:, :2].reshape(-1)
    experts = jnp.arange(n_experts, dtype=jnp.int32)
    hit = flat[:, None] == experts[None, :]
    counts = jnp.sum(hit.astype(jnp.int32), axis=0)
    padded = ((counts + tm - 1) // tm) * tm
    ends = jnp.cumsum(padded)
    starts = ends - padded
    raw_starts = jnp.cumsum(counts) - counts
    order = jnp.argsort(flat, stable=True).astype(jnp.int32)
    sorted_pos = jnp.argsort(order).astype(jnp.int32)
    pos = sorted_pos + jnp.sum(jnp.where(hit, (starts - raw_starts)[None, :], 0), axis=1)
    max_tiles = (2 * n) // tm + n_experts + FFN_GATHER_SLOTS - 1
    tile_start = jnp.arange(max_tiles, dtype=jnp.int32) * tm
    tile_expert = jnp.minimum(jnp.sum((ends[None, :] <= tile_start[:, None]).astype(jnp.int32), axis=1),
                              n_experts - 1)
    tile_src = jnp.minimum(raw_starts[tile_expert] + tile_start - starts[tile_expert], 2 * n)
    order = jnp.concatenate([order, jnp.arange(tm, dtype=jnp.int32) * 2])
    n_tiles = (ends[-1] // tm).astype(jnp.int32).reshape(1)
    return order, pos.astype(jnp.int32), tile_expert, tile_src.astype(jnp.int32), n_tiles


def _row_copy(src_hbm, dst, src_row, r, sem):
    return pltpu.make_async_copy(src_hbm.at[pl.ds(src_row, 1)], dst.at[pl.ds(r, 1)], sem)


def _row_gather_start(src_hbm, dst, idx_ref, base, rows, sem, shift=0):
    for r in range(rows):
        _row_copy(src_hbm, dst, idx_ref[base + r] >> shift, r, sem).start()


def _row_gather_wait(src_hbm, dst, rows, sem):
    for r in range(rows):
        _row_copy(src_hbm, dst, 0, r, sem).wait()


def _ffn_kernel(te_ref, ts_ref, order_ref, nt_ref, x_hbm, g_ref, wg_ref, wu_ref, wd_ref, y_ref,
                xbuf, hbuf, sem, *, tm):
    del te_ref
    i = pl.program_id(0)
    nt = nt_ref[0]
    slot = i % FFN_GATHER_SLOTS
    ahead = FFN_GATHER_SLOTS - 1

    def request(tile):
        s = tile % FFN_GATHER_SLOTS
        _row_gather_start(x_hbm, xbuf.at[s], order_ref, ts_ref[tile], tm, sem.at[s], shift=1)

    @pl.when(i == 0)
    def _():
        for t in range(ahead):
            request(t)

    @pl.when(i < nt)
    def _():
        _row_gather_wait(x_hbm, xbuf.at[slot], tm, sem.at[slot])
        hbuf[...] = _rms(xbuf[slot], g_ref[...]).astype(BF16)
        request(i + ahead)
        h = hbuf[...]
        a = (jax.nn.silu(jnp.dot(h, wg_ref[0], preferred_element_type=F32))
             * jnp.dot(h, wu_ref[0], preferred_element_type=F32))
        y_ref[...] = jnp.dot(a.astype(BF16), wd_ref[0], preferred_element_type=F32)

    @pl.when((i >= nt) & (i < nt + ahead))
    def _():
        _row_gather_wait(x_hbm, xbuf.at[slot], tm, sem.at[slot])

    @pl.when(i >= nt)
    def _():
        y_ref[...] = jnp.zeros_like(y_ref)


def moe_ffn(x, g, wg, wu, wd, order, tile_expert, tile_src, n_tiles, *, tm, first_expert):
    n, d = x.shape
    f = wg.shape[2]
    max_tiles = tile_expert.shape[0]
    grid_spec = pltpu.PrefetchScalarGridSpec(
        num_scalar_prefetch=4,
        grid=(max_tiles,),
        in_specs=[pl.BlockSpec(memory_space=pl.ANY),
                  pl.BlockSpec((1, d), lambda i, te, *_: (0, 0)),
                  pl.BlockSpec((1, d, f), lambda i, te, *_: (first_expert + te[i], 0, 0)),
                  pl.BlockSpec((1, d, f), lambda i, te, *_: (first_expert + te[i], 0, 0)),
                  pl.BlockSpec((1, f, d), lambda i, te, *_: (first_expert + te[i], 0, 0))],
        out_specs=pl.BlockSpec((tm, d), lambda i, te, *_: (i, 0)),
        scratch_shapes=[pltpu.VMEM((FFN_GATHER_SLOTS, tm, d), F32), pltpu.VMEM((tm, d), BF16),
                        pltpu.SemaphoreType.DMA((FFN_GATHER_SLOTS,))],
    )
    return pl.pallas_call(
        functools.partial(_ffn_kernel, tm=tm),
        out_shape=jax.ShapeDtypeStruct((max_tiles * tm, d), F32),
        grid_spec=grid_spec,
        compiler_params=_cparams(("arbitrary",)),
        name="moe_ffn",
    )(tile_expert, tile_src, order, n_tiles, x, g.reshape(1, d), wg, wu, wd)


def _combine_kernel(pos_ref, x_ref, gates_ref, y_hbm, *rest, tc, with_norm):
    if with_norm:
        g_ref, o_ref, h_ref, ybuf, sem = rest
    else:
        o_ref, ybuf, sem = rest
    i = pl.program_id(0)
    nsteps = pl.num_programs(0)
    slot = i % 2

    @pl.when(i == 0)
    def _():
        _row_gather_start(y_hbm, ybuf.at[0], pos_ref, 0, 2 * tc, sem.at[0])

    @pl.when(i + 1 < nsteps)
    def _():
        _row_gather_start(y_hbm, ybuf.at[1 - slot], pos_ref, (i + 1) * 2 * tc, 2 * tc, sem.at[1 - slot])

    _row_gather_wait(y_hbm, ybuf.at[slot], 2 * tc, sem.at[slot])
    gates = gates_ref[...]
    out = (x_ref[...] + gates[:, 0:1] * ybuf[slot, pl.ds(0, tc)]
           + gates[:, 1:2] * ybuf[slot, pl.ds(tc, tc)])
    o_ref[...] = out
    if with_norm:
        h_ref[...] = _rms(out, g_ref[...]).astype(h_ref.dtype)


def moe_combine(x, gates, y_rows, pos, next_g, *, tc):
    n, d = x.shape
    nt = n // tc
    with_norm = next_g is not None
    pos_tiled = pos.reshape(nt, tc, 2).transpose(0, 2, 1).reshape(-1)
    row = pl.BlockSpec((tc, d), lambda i, p: (i, 0))
    in_specs = [row, pl.BlockSpec((tc, LANES), lambda i, p: (i, 0)), pl.BlockSpec(memory_space=pl.ANY)]
    args = [pos_tiled, x, gates, y_rows]
    out_shape = jax.ShapeDtypeStruct((n, d), F32)
    out_specs = row
    if with_norm:
        in_specs.append(pl.BlockSpec((1, d), lambda i, p: (0, 0)))
        args.append(next_g.reshape(1, d))
        out_shape = (out_shape, jax.ShapeDtypeStruct((n, d), BF16))
        out_specs = (row, row)
    grid_spec = pltpu.PrefetchScalarGridSpec(
        num_scalar_prefetch=1,
        grid=(nt,),
        in_specs=in_specs,
        out_specs=out_specs,
        scratch_shapes=[pltpu.VMEM((2, 2 * tc, d), F32), pltpu.SemaphoreType.DMA((2,))],
    )
    res = pl.pallas_call(
        functools.partial(_combine_kernel, tc=tc, with_norm=with_norm),
        out_shape=out_shape,
        grid_spec=grid_spec,
        compiler_params=_cparams(("arbitrary",)),
        name="moe_combine",
    )(*args)
    return res if with_norm else (res, None)


def hier_moe(x, g, w_rg, b_rg, w_re, b_re, wg, wu, wd, *, first_expert, n_experts, next_g):
    n = x.shape[0]
    mean_load = max(1, 2 * n // n_experts)
    tm = min(256, max(32, 1 << (mean_load - 1).bit_length()))
    assert tm <= n
    ids, gates = moe_router(x, g, w_rg, b_rg, w_re, b_re)
    order, pos, tile_expert, tile_src, n_tiles = moe_dispatch(ids, n_experts, tm)
    y_rows = moe_ffn(x, g, wg, wu, wd, order, tile_expert, tile_src, n_tiles, tm=tm,
                     first_expert=first_expert)
    return moe_combine(x, gates, y_rows, pos, next_g, tc=_pick(n, 128))


def _cumsum_kernel(x_ref, c_ref, ct_ref, carry, *, L):
    @pl.when(pl.program_id(1) == 0)
    def _():
        carry[...] = jnp.zeros_like(carry)

    tri = (lax.broadcasted_iota(jnp.int32, (L, L), 0)
           >= lax.broadcasted_iota(jnp.int32, (L, L), 1)).astype(F32)
    c = jnp.dot(tri, x_ref[0], preferred_element_type=F32,
                precision=lax.Precision.HIGHEST) + carry[...]
    c_ref[0] = c * LOG2E
    ct_ref[0] = (c * LOG2E).T
    carry[...] = c[L - 1:L, :]


def cumsum_time(x):
    B, T, w = x.shape
    L = _pick(T, 256)
    return pl.pallas_call(
        functools.partial(_cumsum_kernel, L=L),
        out_shape=(jax.ShapeDtypeStruct((B, T, w), F32), jax.ShapeDtypeStruct((B, w, T), F32)),
        grid=(B, T // L),
        in_specs=[pl.BlockSpec((1, L, w), lambda b, t: (b, t, 0))],
        out_specs=(pl.BlockSpec((1, L, w), lambda b, t: (b, t, 0)),
                   pl.BlockSpec((1, w, L), lambda b, t: (b, 0, t))),
        scratch_shapes=[pltpu.VMEM((1, w), F32)],
        compiler_params=_cparams(("parallel", "arbitrary")),
        name="cumsum_time",
    )(x)


def _row_bcast(r, width):
    if width % LANES == 0:
        return jnp.tile(r, (1, width // LANES))
    return r[:, :1]


def _softmax_stats(t, cq, m_old, l_old):
    m_new = jnp.maximum(m_old, jnp.max(t, axis=-1, keepdims=True) + cq)
    alpha = jnp.exp2(m_old - m_new)
    p = jnp.exp2(t + _row_bcast(cq - m_new, t.shape[1]))
    l_new = alpha * l_old + jnp.sum(p, axis=-1, keepdims=True)
    return m_new, l_new, alpha, p


def _fox_prompt_kernel(q_ref, k_ref, v_ref, cq_ref, ck_ref, o_ref, m_scr, l_scr, acc_scr, cq_scr,
                       *, HG, tq, tk, hd):
    hg = pl.program_id(1)
    qi = pl.program_id(2)
    ki = pl.program_id(3)

    @pl.when(ki == 0)
    def _():
        m_scr[...] = jnp.full_like(m_scr, -jnp.inf)
        l_scr[...] = jnp.zeros_like(l_scr)
        acc_scr[...] = jnp.zeros_like(acc_scr)
        cq_all = cq_ref[...]
        lane = lax.broadcasted_iota(jnp.int32, cq_all.shape, 1)
        for hh in range(HG):
            col = jnp.sum(jnp.where(lane == hg * HG + hh, cq_all, 0.0), axis=-1, keepdims=True)
            cq_scr[hh] = jnp.broadcast_to(col, cq_all.shape)

    def run(masked):
        if masked:
            keep = (lax.broadcasted_iota(jnp.int32, (tq, tk), 0) + qi * tq
                    >= lax.broadcasted_iota(jnp.int32, (tq, tk), 1) + ki * tk)

        def head(hh, carry):
            s = lax.dot_general(q_ref[0, hh], k_ref[0, hh], (((1,), (1,)), ((), ())),
                                preferred_element_type=F32)
            t = s - ck_ref[0, pl.ds(hh, 1), :]
            if masked:
                t = jnp.where(keep, t, -jnp.inf)
            m_new, l_new, alpha, p = _softmax_stats(t, cq_scr[hh], m_scr[hh], l_scr[hh])
            m_scr[hh] = m_new
            l_scr[hh] = l_new
            acc_scr[hh] = alpha * acc_scr[hh] + jnp.dot(p.astype(BF16), v_ref[0, hh],
                                                        preferred_element_type=F32)
            return carry

        lax.fori_loop(0, HG, head, 0, unroll=True)

    @pl.when(ki < qi)
    def _():
        run(False)

    @pl.when(ki == qi)
    def _():
        run(True)
        for hh in range(HG):
            o_ref[:, hh * hd:(hh + 1) * hd] = (acc_scr[hh] / l_scr[hh]).astype(o_ref.dtype)


def fox_prompt(q_hm, k_hm, v_hm, c_tok, c_t, *, B, T, H, hd):
    HG = _pick(H, 16)
    tq = tk = _pick(T, 512)
    nq = T // tq
    kv_spec = pl.BlockSpec((1, HG, tk, hd), lambda b, g, qi, ki: (b, g, jnp.minimum(ki, qi), 0))
    return pl.pallas_call(
        functools.partial(_fox_prompt_kernel, HG=HG, tq=tq, tk=tk, hd=hd),
        out_shape=jax.ShapeDtypeStruct((B * T, H * hd), BF16),
        grid=(B, H // HG, nq, nq),
        in_specs=[pl.BlockSpec((1, HG, tq, hd), lambda b, g, qi, ki: (b, g, qi, 0)),
                  kv_spec, kv_spec,
                  pl.BlockSpec((tq, LANES), lambda b, g, qi, ki: (b * nq + qi, 0)),
                  pl.BlockSpec((1, HG, tk), lambda b, g, qi, ki: (b, g, jnp.minimum(ki, qi)))],
        out_specs=pl.BlockSpec((tq, HG * hd), lambda b, g, qi, ki: (b * nq + qi, g)),
        scratch_shapes=[pltpu.VMEM((HG, tq, LANES), F32), pltpu.VMEM((HG, tq, LANES), F32),
                        pltpu.VMEM((HG, tq, hd), F32), pltpu.VMEM((HG, tq, LANES), F32)],
        compiler_params=_cparams(("parallel", "parallel", "parallel", "arbitrary")),
        name="fox_prompt",
    )(q_hm, k_hm, v_hm, c_tok, c_t)


def _fox_sample_kernel(q_ref, kn_ref, vn_ref, k_hbm, v_hbm, cq_ref, cn_ref, cp_ref, o_ref,
                       cq_scr, m_scr, l_scr, acc_scr, s_scr, p_scr, kbuf, vbuf, sem, *, H, hd, Ts, tk, layer):
    b = pl.program_id(0)
    j = pl.program_id(1)

    def tile_copies(tile, slot):
        span = pl.ds(tile * tk, tk)
        for hh in range(H):
            yield pltpu.make_async_copy(k_hbm.at[layer, b, span, hh, :], kbuf.at[slot, hh], sem.at[slot, 0])
            yield pltpu.make_async_copy(v_hbm.at[layer, b, span, hh, :], vbuf.at[slot, hh], sem.at[slot, 1])

    def rows(hh):
        return slice(hh * Ts, (hh + 1) * Ts)

    def cols(hh):
        return slice(hh * hd, (hh + 1) * hd)

    def update(width, key, value, ck, keep):
        for hh in range(H):
            s = lax.dot_general(q_ref[:, cols(hh)], key(hh), (((1,), (1,)), ((), ())),
                                preferred_element_type=F32)
            t = s - ck(hh)
            s_scr[rows(hh), :width] = t if keep is None else jnp.where(keep, t, -jnp.inf)
        m_new, l_new, alpha, p = _softmax_stats(s_scr[:, :width], cq_scr[...], m_scr[...], l_scr[...])
        m_scr[...] = m_new
        l_scr[...] = l_new
        p_scr[:, :width] = p.astype(BF16)
        for hh in range(H):
            acc_scr[rows(hh)] = alpha[rows(hh)] * acc_scr[rows(hh)] + jnp.dot(
                p_scr[rows(hh), :width], value(hh), preferred_element_type=F32)

    @pl.when(j == 0)
    def _():
        for cp in tile_copies(0, 0):
            cp.start()
        m_scr[...] = jnp.full_like(m_scr, -jnp.inf)
        l_scr[...] = jnp.zeros_like(l_scr)
        acc_scr[...] = jnp.zeros_like(acc_scr)
        for hh in range(H):
            cq_scr[rows(hh)] = jnp.broadcast_to(cq_ref[0, :, hh:hh + 1], (Ts, LANES))
        keep = (lax.broadcasted_iota(jnp.int32, (Ts, Ts), 0) >= lax.broadcasted_iota(jnp.int32, (Ts, Ts), 1))
        update(Ts, lambda hh: kn_ref[:, cols(hh)].astype(BF16), lambda hh: vn_ref[:, cols(hh)].astype(BF16),
               lambda hh: cn_ref[0, hh:hh + 1, :Ts], keep)

    @pl.when(j > 0)
    def _():
        tile = j - 1
        slot = tile % 2
        for cp in tile_copies(tile, slot):
            cp.wait()

        @pl.when(j < pl.num_programs(1) - 1)
        def _():
            for cp in tile_copies(tile + 1, 1 - slot):
                cp.start()

        update(tk, lambda hh: kbuf[slot, hh].astype(BF16), lambda hh: vbuf[slot, hh].astype(BF16),
               lambda hh: cp_ref[0, hh:hh + 1, :], None)

    @pl.when(j == pl.num_programs(1) - 1)
    def _():
        for hh in range(H):
            o_ref[:, cols(hh)] = (acc_scr[rows(hh)] / l_scr[rows(hh)]).astype(o_ref.dtype)


def fox_sample(q, k_new, v_new, k_past, v_past, c_tok, c_t, *, B, Ts, P, H, hd, layer):
    w = H * hd
    tk = _pick(P, 256)
    npk = P // tk
    assert P % LANES == 0 and Ts <= LANES and P % Ts == 0 and hd == LANES
    past = pl.BlockSpec(memory_space=pl.ANY)
    new = pl.BlockSpec((Ts, w), lambda b, j: (b, 0))
    return pl.pallas_call(
        functools.partial(_fox_sample_kernel, H=H, hd=hd, Ts=Ts, tk=tk, layer=layer),
        out_shape=jax.ShapeDtypeStruct((B * Ts, w), BF16),
        grid=(B, npk + 1),
        in_specs=[new, new, new, past, past,
                  pl.BlockSpec((1, Ts, LANES), lambda b, j: (b, P // Ts, 0)),
                  pl.BlockSpec((1, LANES, LANES), lambda b, j: (b, 0, P // LANES)),
                  pl.BlockSpec((1, LANES, tk), lambda b, j: (b, 0, jnp.maximum(j - 1, 0)))],
        out_specs=new,
        scratch_shapes=[pltpu.VMEM((H * Ts, LANES), F32), pltpu.VMEM((H * Ts, LANES), F32),
                        pltpu.VMEM((H * Ts, LANES), F32), pltpu.VMEM((H * Ts, hd), F32),
                        pltpu.VMEM((H * Ts, tk), F32), pltpu.VMEM((H * Ts, tk), BF16),
                        pltpu.VMEM((2, H, tk, hd), F32), pltpu.VMEM((2, H, tk, hd), F32),
                        pltpu.SemaphoreType.DMA((2, 2))],
        compiler_params=_cparams(("arbitrary", "arbitrary")),
        name="fox_sample",
    )(q, k_new, v_new, k_past, v_past, c_tok, c_t, c_t)


def fox_project(h, w_in, layer, wf, b_f, qn_g, kn_g, *, B, T, H, hd, head_major):
    m, _ = h.shape
    w = H * hd
    tm, tn = (_pick(T, 1024), _pick(w, 512)) if head_major else _mm_tiles(m, w)
    tpb = T // tm if head_major else 1
    tok = pl.BlockSpec((tm, tn), lambda i, j: (i, j))
    hm = pl.BlockSpec((1, tn // hd, tm, hd), lambda i, j: (i // tpb, j, i % tpb, 0))
    gain = pl.BlockSpec((1, hd), lambda i, j: (0, 0))
    tok_f32 = jax.ShapeDtypeStruct((m, w), F32)
    tok_bf16 = jax.ShapeDtypeStruct((m, w), BF16)
    hm_bf16 = jax.ShapeDtypeStruct((B, H, T, hd), BF16)

    def proj(section, g, norm, scale, shapes, specs, kinds, name):
        ep = functools.partial(_ep_heads, hd=hd, norm=norm, scale=scale, kinds=kinds)
        extras = (g.reshape(1, hd),) if norm else ()
        especs = (gain,) if norm else ()
        return matmul(h, w_in, layer=layer, col0=section * w, n=w, epilogue=ep, extras=extras,
                      extra_specs=especs, out_shapes=shapes, out_specs=specs, tm=tm, tn=tn, name=name)

    q_scale = hd ** -0.5 * LOG2E
    if head_major:
        (q,) = proj(0, qn_g, True, q_scale, (hm_bf16,), (hm,), ("head",), "fox_q")
        k, k_hm = proj(1, kn_g, True, 1.0, (tok_f32, hm_bf16), (tok, hm), ("tok", "head"), "fox_k")
        v, v_hm = proj(2, None, False, 1.0, (tok_f32, hm_bf16), (tok, hm), ("tok", "head"), "fox_v")
    else:
        (q,) = proj(0, qn_g, True, q_scale, (tok_bf16,), (tok,), ("tok",), "fox_q")
        (k,) = proj(1, kn_g, True, 1.0, (tok_f32,), (tok,), ("tok",), "fox_k")
        (v,) = proj(2, None, False, 1.0, (tok_f32,), (tok,), ("tok",), "fox_v")
        k_hm = v_hm = None
    tmf = _pick(m, 1024)
    logf = matmul(h, wf, layer=0, n=LANES, epilogue=_ep_logsig, extras=(b_f,),
                  extra_specs=(pl.BlockSpec((1, LANES), lambda i, j: (0, 0)),),
                  out_shapes=jax.ShapeDtypeStruct((m, LANES), F32),
                  out_specs=pl.BlockSpec((tmf, LANES), lambda i, j: (i, 0)),
                  tm=tmf, tn=LANES, name="fox_logf")
    return q, k, v, logf, k_hm, v_hm


def _stack(parts):
    return parts[0][None] if len(parts) == 1 else jnp.stack(parts)


def _rope_tables(pos, hd):
    half = hd // 2
    inv = ROPE_BASE ** (-jnp.arange(half, dtype=F32) / half)
    ang = pos.astype(F32)[:, None] * inv[None, :]
    return jnp.cos(ang), jnp.sin(ang)


def kernel(x_prompt, x_sample, mem_prompt, state_ret, cache_fox_k, cache_fox_v, cache_fox_logf, cache_mem_k, cache_mem_v, norm_mix, norm_mem, norm_moe, even_w_in, ret_gn_g, cmlp_ln_g, cmlp_ln_b, cmlp_ws, cmlp_bs, even_w_out, odd_w_in, fox_b_f, fox_qn_g, fox_kn_g, odd_w_out, mem_src_g, mem_w_q, mem_w_k, mem_w_v, mem_qn_g, mem_kn_g, mem_w_o, moe_w_rg, moe_b_rg, moe_w_re, moe_b_re, moe_w_gate, moe_w_up, moe_w_down):
    Bp, Tp, D = x_prompt.shape
    Bs, Ts, _ = x_sample.shape
    depth = norm_mix.shape[0]
    P = cache_fox_k.shape[2]
    RH, RHD = state_ret.shape[2], state_ret.shape[3]
    RW = RH * RHD
    CW = cmlp_ln_g.shape[1]
    FH, FHD = cache_fox_k.shape[3], cache_fox_k.shape[4]
    FW = FH * FHD
    MH, MHD = cache_mem_k.shape[3], cache_mem_k.shape[4]
    MLEN = mem_prompt.shape[1]
    NG, NE, _, DF = moe_w_gate.shape[1:]
    assert (4 * RW) % CW == 0 and FHD == LANES and FH <= LANES

    xp = x_prompt.reshape(Bp * Tp, D)
    xs = x_sample.reshape(Bs * Ts, D)
    mem2d = mem_prompt.reshape(Bp * MLEN, D)
    cos_p, sin_p = _rope_tables(jnp.arange(Tp), RHD)
    cos_s, sin_s = _rope_tables(P + jnp.arange(Ts), RHD)

    wg_all = moe_w_gate.reshape(depth * NG * NE, D, DF).astype(BF16)
    wu_all = moe_w_up.reshape(depth * NG * NE, D, DF).astype(BF16)
    wd_all = moe_w_down.reshape(depth * NG * NE, DF, D).astype(BF16)

    ret_S_p, ret_S_s, cmlp_v_s = [], [], []
    fk_p, fv_p, fl_p, fk_s, fv_s, fl_s = [], [], [], [], [], []
    mk_p, mv_p = [], []
    for i in range(depth):
        j = i // 2
        if i == 0:
            hp = rmsnorm_rows(xp, norm_mix[i])
            hs = rmsnorm_rows(xs, norm_mix[i])
        if i % 2 == 0:
            def even(h, x, cos, sin, s0, B, T, want_zv, head_major):
                if head_major:
                    tm, tn = _pick(T, 1024), _pick(4 * RW, 512)
                    tpb = T // tm
                    (proj_ret,) = matmul(
                        h, even_w_in, layer=j, n=4 * RW,
                        epilogue=functools.partial(_ep_heads, hd=RHD, norm=False, scale=1.0, kinds=("head",)),
                        out_shapes=(jax.ShapeDtypeStruct((B, 4 * RH, T, RHD), F32),),
                        out_specs=(pl.BlockSpec((1, tn // RHD, tm, RHD),
                                                lambda a, b: (a // tpb, b, a % tpb, 0)),),
                        tm=tm, tn=tn, name="even_in_ret")
                else:
                    proj_ret = matmul_plain(h, even_w_in, F32, "even_in_ret", layer=j, n=4 * RW)
                proj_mlp = matmul_plain(h, even_w_in, F32, "even_in_mlp", layer=j, col0=4 * RW)
                y_a, S = retention(proj_ret, cos, sin, ret_gn_g[j], s0, B=B, T=T, H=RH, hd=RHD)
                y_b, zv = cmlp(proj_mlp, cmlp_ln_g[j], cmlp_ln_b[j], cmlp_ws[j], cmlp_bs[j],
                               B=B, T=T, u_col=0, want_zv=want_zv)
                return matmul2_residual(y_a, y_b, even_w_out, x, "even_out", layer=j), S, zv

            xp, Sp, _ = even(hp, xp, cos_p, sin_p, None, Bp, Tp, False, True)
            xs, Ss, zvs = even(hs, xs, cos_s, sin_s, state_ret[j], Bs, Ts, True, False)
            ret_S_p.append(Sp)
            ret_S_s.append(Ss)
            cmlp_v_s.append(zvs.reshape(Bs, Ts, CW))
        else:
            wf = jnp.pad(odd_w_in[j, :, 3 * FW:], ((0, 0), (0, LANES - FH)))[None]
            b_f = jnp.pad(fox_b_f[j], (0, LANES - FH)).reshape(1, LANES)
            q, k, v, logf, k_hm, v_hm = fox_project(hp, odd_w_in, j, wf, b_f, fox_qn_g[j], fox_kn_g[j],
                                                    B=Bp, T=Tp, H=FH, hd=FHD, head_major=True)
            c_tok, c_t = cumsum_time(logf.reshape(Bp, Tp, LANES))
            o = fox_prompt(q, k_hm, v_hm, c_tok.reshape(Bp * Tp, LANES), c_t[:, :FH], B=Bp, T=Tp, H=FH, hd=FHD)
            xp = matmul_plain(o, odd_w_out, F32, "odd_out", layer=j, residual=xp)
            fk_p.append(k.reshape(Bp, Tp, FH, FHD))
            fv_p.append(v.reshape(Bp, Tp, FH, FHD))
            fl_p.append(logf[:, :FH].reshape(Bp, Tp, FH))
            q, k, v, logf, _, _ = fox_project(hs, odd_w_in, j, wf, b_f, fox_qn_g[j], fox_kn_g[j],
                                              B=Bs, T=Ts, H=FH, hd=FHD, head_major=False)
            t_pad = -(-(P + Ts) // 256) * 256
            seq = jnp.concatenate([jnp.pad(cache_fox_logf[j], ((0, 0), (0, 0), (0, LANES - FH))),
                                   logf.reshape(Bs, Ts, LANES),
                                   jnp.zeros((Bs, t_pad - P - Ts, LANES), F32)], axis=1)
            c_tok, c_t = cumsum_time(seq)
            o = fox_sample(q, k, v, cache_fox_k, cache_fox_v, c_tok, c_t,
                           B=Bs, Ts=Ts, P=P, H=FH, hd=FHD, layer=j)
            xs = matmul_plain(o, odd_w_out, F32, "odd_out", layer=j, residual=xs)
            fk_s.append(k.reshape(Bs, Ts, FH, FHD))
            fv_s.append(v.reshape(Bs, Ts, FH, FHD))
            fl_s.append(logf[:, :FH].reshape(Bs, Ts, FH))
        m_n = rmsnorm_rows(mem2d, mem_src_g[i])
        mw = MH * MHD
        tmm, tnm = _pick(Bp * MLEN, 1024), _pick(mw, 512)
        tile = pl.BlockSpec((tmm, tnm), lambda a, b: (a, b))
        (mk,) = matmul(m_n, mem_w_k, layer=i, n=mw,
                       epilogue=functools.partial(_ep_heads, hd=MHD, norm=True, scale=1.0, kinds=("tok",)),
                       extras=(mem_kn_g[i].reshape(1, MHD),),
                       extra_specs=(pl.BlockSpec((1, MHD), lambda a, b: (0, 0)),),
                       out_shapes=(jax.ShapeDtypeStruct((Bp * MLEN, mw), F32),), out_specs=(tile,),
                       tm=tmm, tn=tnm, name="mem_k")
        mv = matmul_plain(m_n, mem_w_v, F32, "mem_v", layer=i)
        mk_p.append(mk.reshape(Bp, MLEN, MH, MHD))
        mv_p.append(mv.reshape(Bp, MLEN, MH, MHD))
        wq_m = mem_w_q[i].astype(BF16)
        wo_m = mem_w_o[i].astype(BF16)
        xp = mem_attn(xp, norm_mem[i], wq_m, mem_qn_g[i], mk.reshape(Bp, MLEN, mw), mv.reshape(Bp, MLEN, mw),
                      wo_m, B=Bp, T=Tp, heads=MH)
        xs = mem_attn(xs, norm_mem[i], wq_m, mem_qn_g[i], cache_mem_k[i].reshape(Bs, MLEN, mw),
                      cache_mem_v[i].reshape(Bs, MLEN, mw), wo_m, B=Bs, T=Ts, heads=MH)
        moe = (norm_moe[i], moe_w_rg[i], moe_b_rg[i], moe_w_re[i], moe_b_re[i], wg_all, wu_all, wd_all)
        next_g = norm_mix[i + 1] if i + 1 < depth else None
        xp, hp = hier_moe(xp, *moe, first_expert=i * NG * NE, n_experts=NG * NE, next_g=next_g)
        xs, hs = hier_moe(xs, *moe, first_expert=i * NG * NE, n_experts=NG * NE, next_g=next_g)

    return (xp.reshape(Bp, Tp, D), xs.reshape(Bs, Ts, D),
            _stack(ret_S_p), _stack(ret_S_s), _stack(cmlp_v_s),
            _stack(fk_p), _stack(fv_p), _stack(fl_p),
            _stack(fk_s), _stack(fv_s), _stack(fl_s),
            _stack(mk_p), _stack(mv_p))
```

```python
import functools

import jax
import jax.numpy as jnp
from jax import lax
from jax.experimental import pallas as pl
from jax.experimental.pallas import tpu as pltpu

F32 = jnp.float32
BF16 = jnp.bfloat16
EPS = 1e-6
ROPE_BASE = 10000.0
LOG2E = 1.4426950408889634
LANES = 128
VMEM_LIMIT_BYTES = 56 * 1024 * 1024
FFN_GATHER_SLOTS = 3


def _cparams(semantics):
    return pltpu.CompilerParams(dimension_semantics=semantics,
                                vmem_limit_bytes=VMEM_LIMIT_BYTES)


def _pick(n, pref):
    t = min(n, pref)
    while n % t:
        t //= 2
    assert t >= 1
    return t


def _rms(x, g):
    return x * lax.rsqrt(jnp.mean(x * x, axis=-1, keepdims=True) + EPS) * g


def _rmsnorm_kernel(x_ref, g_ref, o_ref):
    o_ref[...] = _rms(x_ref[...], g_ref[...]).astype(o_ref.dtype)


def rmsnorm_rows(x, g, out_dtype=BF16):
    n, d = x.shape
    tm = _pick(n, 512)
    return pl.pallas_call(
        _rmsnorm_kernel,
        out_shape=jax.ShapeDtypeStruct((n, d), out_dtype),
        grid=(n // tm,),
        in_specs=[pl.BlockSpec((tm, d), lambda i: (i, 0)),
                  pl.BlockSpec((1, d), lambda i: (0, 0))],
        out_specs=pl.BlockSpec((tm, d), lambda i: (i, 0)),
        compiler_params=_cparams(("parallel",)),
        name="rmsnorm_rows",
    )(x, g.reshape(1, d))


def _mm_kernel(*refs, n_extra, epilogue):
    a_ref, w_ref = refs[0], refs[1]
    extra = refs[2:2 + n_extra]
    outs = refs[2 + n_extra:]
    acc = jnp.dot(a_ref[...], w_ref[...].astype(BF16), preferred_element_type=F32)
    epilogue(acc, extra, outs)


def _ep_store(acc, extra, outs):
    outs[0][...] = acc.astype(outs[0].dtype)


def _ep_residual(acc, extra, outs):
    outs[0][...] = extra[0][...] + acc


def _ep_logsig(acc, extra, outs):
    outs[0][...] = jax.nn.log_sigmoid(acc + extra[0][...])


def _ep_heads(acc, extra, outs, *, hd, norm, scale, kinds):
    tn = acc.shape[1]
    for hh in range(tn // hd):
        blk = acc[:, hh * hd:(hh + 1) * hd]
        if norm:
            blk = _rms(blk, extra[0][...])
        if scale != 1.0:
            blk = blk * scale
        for o_ref, kind in zip(outs, kinds):
            if kind == "tok":
                o_ref[:, hh * hd:(hh + 1) * hd] = blk.astype(o_ref.dtype)
            else:
                o_ref[0, hh] = blk.astype(o_ref.dtype)


def _mm_tiles(m, n):
    tm = _pick(m, 1024)
    return tm, _pick(n, 512 if m > tm or tm > 256 else 1024)


def matmul(a, w, *, layer, col0=0, n, epilogue=_ep_store, extras=(), extra_specs=(), out_shapes,
           out_specs, tm, tn, name):
    m, k = a.shape
    assert w.shape[1] == k and m % tm == 0 and n % tn == 0 and col0 % tn == 0
    kern = functools.partial(_mm_kernel, n_extra=len(extras), epilogue=epilogue)
    return pl.pallas_call(
        kern,
        out_shape=out_shapes,
        grid=(m // tm, n // tn),
        in_specs=[pl.BlockSpec((tm, k), lambda i, j: (i, 0)),
                  pl.BlockSpec((None, k, tn), lambda i, j: (layer, 0, j + col0 // tn))] + list(extra_specs),
        out_specs=out_specs,
        compiler_params=_cparams(("parallel", "parallel")),
        name=name,
    )(a, w, *extras)


def matmul_plain(a, w, out_dtype, name, *, layer, col0=0, n=None, residual=None):
    m, _ = a.shape
    n = w.shape[2] - col0 if n is None else n
    tm, tn = _mm_tiles(m, n)
    tile = pl.BlockSpec((tm, tn), lambda i, j: (i, j))
    if residual is None:
        return matmul(a, w, layer=layer, col0=col0, n=n, out_shapes=jax.ShapeDtypeStruct((m, n), out_dtype),
                      out_specs=tile, tm=tm, tn=tn, name=name)
    return matmul(a, w, layer=layer, col0=col0, n=n, epilogue=_ep_residual, extras=(residual,),
                  extra_specs=(tile,), out_shapes=jax.ShapeDtypeStruct((m, n), F32), out_specs=tile,
                  tm=tm, tn=tn, name=name)


def _mm2_residual_kernel(a1_ref, w1_ref, a2_ref, w2_ref, res_ref, o_ref):
    o_ref[...] = (res_ref[...]
                  + jnp.dot(a1_ref[...], w1_ref[...].astype(BF16), preferred_element_type=F32)
                  + jnp.dot(a2_ref[...], w2_ref[...].astype(BF16), preferred_element_type=F32))


def matmul2_residual(a1, a2, w, residual, name, *, layer):
    m, k1 = a1.shape
    k2 = a2.shape[1]
    n = w.shape[2]
    assert k1 == k2 and w.shape[1] == k1 + k2
    tm, tn = _mm_tiles(m, n)
    tile = pl.BlockSpec((tm, tn), lambda i, j: (i, j))
    return pl.pallas_call(
        _mm2_residual_kernel,
        out_shape=jax.ShapeDtypeStruct((m, n), F32),
        grid=(m // tm, n // tn),
        in_specs=[pl.BlockSpec((tm, k1), lambda i, j: (i, 0)),
                  pl.BlockSpec((None, k1, tn), lambda i, j: (layer, 0, j)),
                  pl.BlockSpec((tm, k2), lambda i, j: (i, 0)),
                  pl.BlockSpec((None, k2, tn), lambda i, j: (layer, 1, j)),
                  tile],
        out_specs=tile,
        compiler_params=_cparams(("parallel", "parallel")),
        name=name,
    )(a1, w, a2, w, residual)


def _ret_kernel(lg_ref, gl_ref, q_ref, k_ref, v_ref, gate_ref, cos_ref, sin_ref, gn_ref, *rest,
                L, hd, HB, has_s0, head_major):
    if has_s0:
        s0_ref, y_ref, sout_ref, s_scr, din_scr, qdec_scr, kdec_scr = rest
    else:
        y_ref, sout_ref, s_scr, din_scr, qdec_scr, kdec_scr = rest
    hb = pl.program_id(1)
    c = pl.program_id(2)

    @pl.when(c == 0)
    def _():
        n_col = lax.broadcasted_iota(jnp.int32, (L, 1), 0).astype(F32)
        diff = (lax.broadcasted_iota(jnp.int32, (L, L), 0)
                - lax.broadcasted_iota(jnp.int32, (L, L), 1)).astype(F32)
        for hh in range(HB):
            s_scr[hh] = s0_ref[0, hh] if has_s0 else jnp.zeros((hd, hd), F32)
            lg = lg_ref[hb * HB + hh]
            din_scr[hh] = jnp.where(diff >= 0, jnp.exp(diff * lg), 0.0)
            qdec_scr[hh] = jnp.broadcast_to(jnp.exp((n_col + 1.0) * lg), (L, hd))
            kdec_scr[hh] = jnp.broadcast_to(jnp.exp((L - 1.0 - n_col) * lg), (L, hd))

    half = hd // 2
    cos = cos_ref[...]
    sin = sin_ref[...]

    def rope(x):
        x1, x2 = x[:, :half], x[:, half:]
        return jnp.concatenate([x1 * cos - x2 * sin, x1 * sin + x2 * cos], axis=-1)

    def head(hh):
        cols = slice(hh * hd, (hh + 1) * hd)

        def tile(ref):
            return ref[0, hh] if head_major else ref[:, cols]

        q = rope(tile(q_ref))
        k = rope(tile(k_ref)) * (hd ** -0.5)
        vb = tile(v_ref).astype(BF16)
        qb = q.astype(BF16)
        s = lax.dot_general(qb, k.astype(BF16), (((1,), (1,)), ((), ())),
                            preferred_element_type=F32) * din_scr[hh]
        inner = jnp.dot(s.astype(BF16), vb, preferred_element_type=F32)
        s_old = s_scr[hh]
        cross = jnp.dot(qb, s_old.astype(BF16), preferred_element_type=F32) * qdec_scr[hh]
        kd = (k * kdec_scr[hh]).astype(BF16)
        s_new = gl_ref[hb * HB + hh] * s_old + lax.dot_general(kd, vb, (((0,), (0,)), ((), ())),
                                                               preferred_element_type=F32)
        s_scr[hh] = s_new
        o = inner + cross
        xc = o - jnp.mean(o, axis=-1, keepdims=True)
        on = xc * lax.rsqrt(jnp.mean(xc * xc, axis=-1, keepdims=True) + EPS) * gn_ref[:, cols]
        y_ref[:, cols] = (jax.nn.silu(tile(gate_ref)) * on).astype(y_ref.dtype)

        @pl.when(c == pl.num_programs(2) - 1)
        def _():
            sout_ref[0, hh] = s_new

    for hh in range(HB):
        head(hh)


def retention(proj, cos, sin, gn_g, s0, *, B, T, H, hd):
    L = _pick(T, 256)
    nc = T // L
    HB = _pick(H, 4)
    lg = jnp.log1p(-jnp.exp2(-5.0 - jnp.arange(H, dtype=F32)))
    gl = jnp.exp(L * lg)
    has_s0 = s0 is not None
    head_major = proj.ndim == 4

    def col(sec):
        first = sec * H // HB
        if head_major:
            return pl.BlockSpec((1, HB, L, hd), lambda b, h, c: (b, first + h, c, 0))
        return pl.BlockSpec((L, HB * hd), lambda b, h, c: (b * nc + c, first + h))

    smem = pl.BlockSpec(memory_space=pltpu.SMEM)
    in_specs = [smem, smem, col(0), col(1), col(2), col(3),
                pl.BlockSpec((L, hd // 2), lambda b, h, c: (c, 0)),
                pl.BlockSpec((L, hd // 2), lambda b, h, c: (c, 0)),
                pl.BlockSpec((1, HB * hd), lambda b, h, c: (0, h))]
    args = [lg, gl, proj, proj, proj, proj, cos, sin, gn_g.reshape(1, H * hd)]
    state_spec = pl.BlockSpec((1, HB, hd, hd), lambda b, h, c: (b, h, 0, 0))
    if has_s0:
        in_specs.append(state_spec)
        args.append(s0)
    return pl.pallas_call(
        functools.partial(_ret_kernel, L=L, hd=hd, HB=HB, has_s0=has_s0, head_major=head_major),
        out_shape=(jax.ShapeDtypeStruct((B * T, H * hd), BF16),
                   jax.ShapeDtypeStruct((B, H, hd, hd), F32)),
        grid=(B, H // HB, nc),
        in_specs=in_specs,
        out_specs=(pl.BlockSpec((L, HB * hd), lambda b, h, c: (b * nc + c, h)), state_spec),
        scratch_shapes=[pltpu.VMEM((HB, hd, hd), F32), pltpu.VMEM((HB, L, L), F32),
                        pltpu.VMEM((HB, L, hd), F32), pltpu.VMEM((HB, L, hd), F32)],
        compiler_params=_cparams(("parallel", "parallel", "arbitrary")),
        name="retention",
    )(*args)


def _cmlp_kernel(u_ref, vb_ref, lng_ref, lnb_ref, ws_ref, bst_ref, y_ref, *zv_out, G, gd, L):
    zu = jax.nn.gelu(u_ref[...])
    gv = jax.nn.gelu(vb_ref[...])
    xc = gv - jnp.mean(gv, axis=-1, keepdims=True)
    zv = xc * lax.rsqrt(jnp.mean(xc * xc, axis=-1, keepdims=True) + EPS) * lng_ref[...] + lnb_ref[...]
    if zv_out:
        zv_out[0][...] = zv
    keep = (lax.broadcasted_iota(jnp.int32, (L, L), 0) >= lax.broadcasted_iota(jnp.int32, (L, L), 1))
    for g in range(G):
        w = jnp.where(keep, ws_ref[g, :L, :L], 0.0).astype(BF16)
        mixed = jnp.dot(w, zv[:, g * gd:(g + 1) * gd].astype(BF16), preferred_element_type=F32)
        mixed = mixed + bst_ref[:, g:g + 1]
        y_ref[:, g * gd:(g + 1) * gd] = (zu[:, g * gd:(g + 1) * gd] * mixed).astype(y_ref.dtype)


def cmlp(proj, ln_g, ln_b, ws, bs, *, B, T, u_col, want_zv):
    G, chunk, _ = ws.shape
    W = ln_g.shape[0]
    gd = W // G
    L = min(T, chunk)
    nc = T // L
    out_shape = [jax.ShapeDtypeStruct((B * T, W), BF16)]
    out_specs = [pl.BlockSpec((L, W), lambda b, c: (b * nc + c, 0))]
    if want_zv:
        out_shape.append(jax.ShapeDtypeStruct((B * T, W), F32))
        out_specs.append(pl.BlockSpec((L, W), lambda b, c: (b * nc + c, 0)))
    res = pl.pallas_call(
        functools.partial(_cmlp_kernel, G=G, gd=gd, L=L),
        out_shape=tuple(out_shape),
        grid=(B, nc),
        in_specs=[pl.BlockSpec((L, W), lambda b, c: (b * nc + c, u_col)),
                  pl.BlockSpec((L, W), lambda b, c: (b * nc + c, u_col + 1)),
                  pl.BlockSpec((1, W), lambda b, c: (0, 0)),
                  pl.BlockSpec((1, W), lambda b, c: (0, 0)),
                  pl.BlockSpec((G, chunk, chunk), lambda b, c: (0, 0, 0)),
                  pl.BlockSpec((L, G), lambda b, c: (0, 0))],
        out_specs=tuple(out_specs),
        compiler_params=_cparams(("parallel", "parallel")),
        name="cmlp",
    )(proj, proj, ln_g.reshape(1, W), ln_b.reshape(1, W), ws, bs[:, :L].T)
    return res if want_zv else (res[0], None)


def _mem_kernel(x_ref, g_ref, wq_ref, qn_ref, mk_ref, mv_ref, wo_ref, o_ref, *, heads, hd):
    x = x_ref[...]
    h = _rms(x, g_ref[...]).astype(BF16)
    q = jnp.dot(h, wq_ref[...], preferred_element_type=F32)
    mk = mk_ref[0]
    mv = mv_ref[0]
    outs = []
    for hh in range(heads):
        sl = slice(hh * hd, (hh + 1) * hd)
        qh = _rms(q[:, sl], qn_ref[...]).astype(BF16)
        s = lax.dot_general(qh, mk[:, sl].astype(BF16), (((1,), (1,)), ((), ())),
                            preferred_element_type=F32) * (hd ** -0.5)
        e = jnp.exp(s - jnp.max(s, axis=-1, keepdims=True))
        p = (e / jnp.sum(e, axis=-1, keepdims=True)).astype(BF16)
        outs.append(jnp.dot(p, mv[:, sl].astype(BF16), preferred_element_type=F32))
    o = jnp.concatenate(outs, axis=-1).astype(BF16)
    o_ref[...] = x + jnp.dot(o, wo_ref[...], preferred_element_type=F32)


def mem_attn(x, g, wq, qn_g, mk, mv, wo, *, B, T, heads):
    n, d = x.shape
    mw = wq.shape[1]
    hd = mw // heads
    mlen = mk.shape[1]
    tm = _pick(T, 256)
    tpb = T // tm
    return pl.pallas_call(
        functools.partial(_mem_kernel, heads=heads, hd=hd),
        out_shape=jax.ShapeDtypeStruct((n, d), F32),
        grid=(n // tm,),
        in_specs=[pl.BlockSpec((tm, d), lambda i: (i, 0)),
                  pl.BlockSpec((1, d), lambda i: (0, 0)),
                  pl.BlockSpec((d, mw), lambda i: (0, 0)),
                  pl.BlockSpec((1, hd), lambda i: (0, 0)),
                  pl.BlockSpec((1, mlen, mw), lambda i: (i // tpb, 0, 0)),
                  pl.BlockSpec((1, mlen, mw), lambda i: (i // tpb, 0, 0)),
                  pl.BlockSpec((mw, d), lambda i: (0, 0))],
        out_specs=pl.BlockSpec((tm, d), lambda i: (i, 0)),
        compiler_params=_cparams(("parallel",)),
        name="mem_attn",
    )(x, g.reshape(1, d), wq, qn_g.reshape(1, hd), mk, mv, wo)


def _router_kernel(x_ref, g_ref, w_ref, b_ref, ids_ref, gates_ref, *, G, E):
    h = _rms(x_ref[...], g_ref[...])
    h1 = h.astype(BF16)
    h2 = (h - h1.astype(F32)).astype(BF16)
    a = jnp.dot(h1, w_ref[...], preferred_element_type=F32)
    logits = (a[:, :LANES] + a[:, LANES:] + jnp.dot(h2, w_ref[:, :LANES], preferred_element_type=F32)
              + b_ref[...])
    lane = lax.broadcasted_iota(jnp.int32, logits.shape, 1)
    neg = -jnp.inf
    is_g = lane < G
    lgm = jnp.where(is_g, logits, neg)
    gmax = jnp.max(lgm, axis=-1, keepdims=True)
    gsel = jnp.min(jnp.where(lgm == gmax, lane, LANES), axis=-1, keepdims=True)
    p_grp = 1.0 / jnp.sum(jnp.where(is_g, jnp.exp(logits - gmax), 0.0), axis=-1, keepdims=True)
    lo = G + gsel * E
    lem = jnp.where((lane >= lo) & (lane < lo + E), logits, neg)
    v1 = jnp.max(lem, axis=-1, keepdims=True)
    i1 = jnp.min(jnp.where(lem == v1, lane, LANES), axis=-1, keepdims=True)
    lem2 = jnp.where(lane == i1, neg, lem)
    v2 = jnp.max(lem2, axis=-1, keepdims=True)
    i2 = jnp.min(jnp.where(lem2 == v2, lane, LANES), axis=-1, keepdims=True)
    e2 = jnp.exp(v2 - v1)
    den = 1.0 + e2
    ids_ref[...] = jnp.where(lane == 0, i1 - G, jnp.where(lane == 1, i2 - G, 0))
    gates_ref[...] = jnp.where(lane == 0, p_grp / den, jnp.where(lane == 1, p_grp * e2 / den, 0.0))


def moe_router(x, g, w_rg, b_rg, w_re, b_re):
    n, d = x.shape
    G = w_rg.shape[1]
    E = w_re.shape[1] // G
    assert G + G * E <= LANES
    pad = LANES - G - G * E
    w = jnp.concatenate([w_rg, w_re, jnp.zeros((d, pad), F32)], axis=1)
    w1 = w.astype(BF16)
    w = jnp.concatenate([w1, (w - w1.astype(F32)).astype(BF16)], axis=1)
    b = jnp.concatenate([b_rg, b_re, jnp.zeros((pad,), F32)]).reshape(1, LANES)
    tm = _pick(n, 256)
    row = pl.BlockSpec((tm, LANES), lambda i: (i, 0))
    return pl.pallas_call(
        functools.partial(_router_kernel, G=G, E=E),
        out_shape=(jax.ShapeDtypeStruct((n, LANES), jnp.int32),
                   jax.ShapeDtypeStruct((n, LANES), F32)),
        grid=(n // tm,),
        in_specs=[pl.BlockSpec((tm, d), lambda i: (i, 0)),
                  pl.BlockSpec((1, d), lambda i: (0, 0)),
                  pl.BlockSpec((d, 2 * LANES), lambda i: (0, 0)),
                  pl.BlockSpec((1, LANES), lambda i: (0, 0))],
        out_specs=(row, row),
        compiler_params=_cparams(("parallel",)),
        name="moe_router",
    )(x, g.reshape(1, d), w, b)


def moe_dispatch(ids, n_experts, tm):
    n = ids.shape[0]
    flat = ids[:, :2].reshape(-1)
    experts = jnp.arange(n_experts, dtype=jnp.int32)
    hit = flat[:, None] == experts[None, :]
    counts = jnp.sum(hit.astype(jnp.int32), axis=0)
    padded = ((counts + tm - 1) // tm) * tm
    ends = jnp.cumsum(padded)
    starts = ends - padded
    raw_starts = jnp.cumsum(counts) - counts
    order = jnp.argsort(flat, stable=True).astype(jnp.int32)
    sorted_pos = jnp.argsort(order).astype(jnp.int32)
    pos = sorted_pos + jnp.sum(jnp.where(hit, (starts - raw_starts)[None, :], 0), axis=1)
    max_tiles = (2 * n) // tm + n_experts + FFN_GATHER_SLOTS - 1
    tile_start = jnp.arange(max_tiles, dtype=jnp.int32) * tm
    tile_expert = jnp.minimum(jnp.sum((ends[None, :] <= tile_start[:, None]).astype(jnp.int32), axis=1),
                              n_experts - 1)
    tile_src = jnp.minimum(raw_starts[tile_expert] + tile_start - starts[tile_expert], 2 * n)
    order = jnp.concatenate([order, jnp.arange(tm, dtype=jnp.int32) * 2])
    n_tiles = (ends[-1] // tm).astype(jnp.int32).reshape(1)
    return order, pos.astype(jnp.int32), tile_expert, tile_src.astype(jnp.int32), n_tiles


def _row_copy(src_hbm, dst, src_row, r, sem):
    return pltpu.make_async_copy(src_hbm.at[pl.ds(src_row, 1)], dst.at[pl.ds(r, 1)], sem)


def _row_gather_start(src_hbm, dst, idx_ref, base, rows, sem, shift=0):
    for r in range(rows):
        _row_copy(src_hbm, dst, idx_ref[base + r] >> shift, r, sem).start()


def _row_gather_wait(src_hbm, dst, rows, sem):
    for r in range(rows):
        _row_copy(src_hbm, dst, 0, r, sem).wait()


def _ffn_kernel(te_ref, ts_ref, order_ref, nt_ref, x_hbm, g_ref, wg_ref, wu_ref, wd_ref, y_ref,
                xbuf, hbuf, sem, *, tm):
    del te_ref
    i = pl.program_id(0)
    nt = nt_ref[0]
    slot = i % FFN_GATHER_SLOTS
    ahead = FFN_GATHER_SLOTS - 1

    def request(tile):
        s = tile % FFN_GATHER_SLOTS
        _row_gather_start(x_hbm, xbuf.at[s], order_ref, ts_ref[tile], tm, sem.at[s], shift=1)

    @pl.when(i == 0)
    def _():
        for t in range(ahead):
            request(t)

    @pl.when(i < nt)
    def _():
        _row_gather_wait(x_hbm, xbuf.at[slot], tm, sem.at[slot])
        hbuf[...] = _rms(xbuf[slot], g_ref[...]).astype(BF16)
        request(i + ahead)
        h = hbuf[...]
        a = (jax.nn.silu(jnp.dot(h, wg_ref[0], preferred_element_type=F32))
             * jnp.dot(h, wu_ref[0], preferred_element_type=F32))
        y_ref[...] = jnp.dot(a.astype(BF16), wd_ref[0], preferred_element_type=F32)

    @pl.when((i >= nt) & (i < nt + ahead))
    def _():
        _row_gather_wait(x_hbm, xbuf.at[slot], tm, sem.at[slot])

    @pl.when(i >= nt)
    def _():
        y_ref[...] = jnp.zeros_like(y_ref)


def moe_ffn(x, g, wg, wu, wd, order, tile_expert, tile_src, n_tiles, *, tm, first_expert):
    n, d = x.shape
    f = wg.shape[2]
    max_tiles = tile_expert.shape[0]
    grid_spec = pltpu.PrefetchScalarGridSpec(
        num_scalar_prefetch=4,
        grid=(max_tiles,),
        in_specs=[pl.BlockSpec(memory_space=pl.ANY),
                  pl.BlockSpec((1, d), lambda i, te, *_: (0, 0)),
                  pl.BlockSpec((1, d, f), lambda i, te, *_: (first_expert + te[i], 0, 0)),
                  pl.BlockSpec((1, d, f), lambda i, te, *_: (first_expert + te[i], 0, 0)),
                  pl.BlockSpec((1, f, d), lambda i, te, *_: (first_expert + te[i], 0, 0))],
        out_specs=pl.BlockSpec((tm, d), lambda i, te, *_: (i, 0)),
        scratch_shapes=[pltpu.VMEM((FFN_GATHER_SLOTS, tm, d), F32), pltpu.VMEM((tm, d), BF16),
                        pltpu.SemaphoreType.DMA((FFN_GATHER_SLOTS,))],
    )
    return pl.pallas_call(
        functools.partial(_ffn_kernel, tm=tm),
        out_shape=jax.ShapeDtypeStruct((max_tiles * tm, d), F32),
        grid_spec=grid_spec,
        compiler_params=_cparams(("arbitrary",)),
        name="moe_ffn",
    )(tile_expert, tile_src, order, n_tiles, x, g.reshape(1, d), wg, wu, wd)


def _combine_kernel(pos_ref, x_ref, gates_ref, y_hbm, *rest, tc, with_norm):
    if with_norm:
        g_ref, o_ref, h_ref, ybuf, sem = rest
    else:
        o_ref, ybuf, sem = rest
    i = pl.program_id(0)
    nsteps = pl.num_programs(0)
    slot = i % 2

    @pl.when(i == 0)
    def _():
        _row_gather_start(y_hbm, ybuf.at[0], pos_ref, 0, 2 * tc, sem.at[0])

    @pl.when(i + 1 < nsteps)
    def _():
        _row_gather_start(y_hbm, ybuf.at[1 - slot], pos_ref, (i + 1) * 2 * tc, 2 * tc, sem.at[1 - slot])

    _row_gather_wait(y_hbm, ybuf.at[slot], 2 * tc, sem.at[slot])
    gates = gates_ref[...]
    out = (x_ref[...] + gates[:, 0:1] * ybuf[slot, pl.ds(0, tc)]
           + gates[:, 1:2] * ybuf[slot, pl.ds(tc, tc)])
    o_ref[...] = out
    if with_norm:
        h_ref[...] = _rms(out, g_ref[...]).astype(h_ref.dtype)


def moe_combine(x, gates, y_rows, pos, next_g, *, tc):
    n, d = x.shape
    nt = n // tc
    with_norm = next_g is not None
    pos_tiled = pos.reshape(nt, tc, 2).transpose(0, 2, 1).reshape(-1)
    row = pl.BlockSpec((tc, d), lambda i, p: (i, 0))
    in_specs = [row, pl.BlockSpec((tc, LANES), lambda i, p: (i, 0)), pl.BlockSpec(memory_space=pl.ANY)]
    args = [pos_tiled, x, gates, y_rows]
    out_shape = jax.ShapeDtypeStruct((n, d), F32)
    out_specs = row
    if with_norm:
        in_specs.append(pl.BlockSpec((1, d), lambda i, p: (0, 0)))
        args.append(next_g.reshape(1, d))
        out_shape = (out_shape, jax.ShapeDtypeStruct((n, d), BF16))
        out_specs = (row, row)
    grid_spec = pltpu.PrefetchScalarGridSpec(
        num_scalar_prefetch=1,
        grid=(nt,),
        in_specs=in_specs,
        out_specs=out_specs,
        scratch_shapes=[pltpu.VMEM((2, 2 * tc, d), F32), pltpu.SemaphoreType.DMA((2,))],
    )
    res = pl.pallas_call(
        functools.partial(_combine_kernel, tc=tc, with_norm=with_norm),
        out_shape=out_shape,
        grid_spec=grid_spec,
        compiler_params=_cparams(("arbitrary",)),
        name="moe_combine",
    )(*args)
    return res if with_norm else (res, None)


def hier_moe(x, g, w_rg, b_rg, w_re, b_re, wg, wu, wd, *, first_expert, n_experts, next_g):
    n = x.shape[0]
    mean_load = max(1, 2 * n // n_experts)
    tm = min(256, max(32, 1 << (mean_load - 1).bit_length()))
    assert tm <= n
    ids, gates = moe_router(x, g, w_rg, b_rg, w_re, b_re)
    order, pos, tile_expert, tile_src, n_tiles = moe_dispatch(ids, n_experts, tm)
    y_rows = moe_ffn(x, g, wg, wu, wd, order, tile_expert, tile_src, n_tiles, tm=tm,
                     first_expert=first_expert)
    return moe_combine(x, gates, y_rows, pos, next_g, tc=_pick(n, 128))


def _cumsum_kernel(x_ref, c_ref, ct_ref, carry, *, L):
    @pl.when(pl.program_id(1) == 0)
    def _():
        carry[...] = jnp.zeros_like(carry)

    tri = (lax.broadcasted_iota(jnp.int32, (L, L), 0)
           >= lax.broadcasted_iota(jnp.int32, (L, L), 1)).astype(F32)
    c = jnp.dot(tri, x_ref[0], preferred_element_type=F32,
                precision=lax.Precision.HIGHEST) + carry[...]
    c_ref[0] = c * LOG2E
    ct_ref[0] = (c * LOG2E).T
    carry[...] = c[L - 1:L, :]


def cumsum_time(x):
    B, T, w = x.shape
    L = _pick(T, 256)
    return pl.pallas_call(
        functools.partial(_cumsum_kernel, L=L),
        out_shape=(jax.ShapeDtypeStruct((B, T, w), F32), jax.ShapeDtypeStruct((B, w, T), F32)),
        grid=(B, T // L),
        in_specs=[pl.BlockSpec((1, L, w), lambda b, t: (b, t, 0))],
        out_specs=(pl.BlockSpec((1, L, w), lambda b, t: (b, t, 0)),
                   pl.BlockSpec((1, w, L), lambda b, t: (b, 0, t))),
        scratch_shapes=[pltpu.VMEM((1, w), F32)],
        compiler_params=_cparams(("parallel", "arbitrary")),
        name="cumsum_time",
    )(x)


def _row_bcast(r, width):
    if width % LANES == 0:
        return jnp.tile(r, (1, width // LANES))
    return r[:, :1]


def _softmax_stats(t, cq, m_old, l_old):
    m_new = jnp.maximum(m_old, jnp.max(t, axis=-1, keepdims=True) + cq)
    alpha = jnp.exp2(m_old - m_new)
    p = jnp.exp2(t + _row_bcast(cq - m_new, t.shape[1]))
    l_new = alpha * l_old + jnp.sum(p, axis=-1, keepdims=True)
    return m_new, l_new, alpha, p


def _fox_prompt_kernel(q_ref, k_ref, v_ref, cq_ref, ck_ref, o_ref, m_scr, l_scr, acc_scr, cq_scr,
                       *, HG, tq, tk, hd):
    hg = pl.program_id(1)
    qi = pl.program_id(2)
    ki = pl.program_id(3)

    @pl.when(ki == 0)
    def _():
        m_scr[...] = jnp.full_like(m_scr, -jnp.inf)
        l_scr[...] = jnp.zeros_like(l_scr)
        acc_scr[...] = jnp.zeros_like(acc_scr)
        cq_all = cq_ref[...]
        lane = lax.broadcasted_iota(jnp.int32, cq_all.shape, 1)
        for hh in range(HG):
            col = jnp.sum(jnp.where(lane == hg * HG + hh, cq_all, 0.0), axis=-1, keepdims=True)
            cq_scr[hh] = jnp.broadcast_to(col, cq_all.shape)

    def run(masked):
        if masked:
            keep = (lax.broadcasted_iota(jnp.int32, (tq, tk), 0) + qi * tq
                    >= lax.broadcasted_iota(jnp.int32, (tq, tk), 1) + ki * tk)

        def head(hh, carry):
            s = lax.dot_general(q_ref[0, hh], k_ref[0, hh], (((1,), (1,)), ((), ())),
                                preferred_element_type=F32)
            t = s - ck_ref[0, pl.ds(hh, 1), :]
            if masked:
                t = jnp.where(keep, t, -jnp.inf)
            m_new, l_new, alpha, p = _softmax_stats(t, cq_scr[hh], m_scr[hh], l_scr[hh])
            m_scr[hh] = m_new
            l_scr[hh] = l_new
            acc_scr[hh] = alpha * acc_scr[hh] + jnp.dot(p.astype(BF16), v_ref[0, hh],
                                                        preferred_element_type=F32)
            return carry

        lax.fori_loop(0, HG, head, 0, unroll=True)

    @pl.when(ki < qi)
    def _():
        run(False)

    @pl.when(ki == qi)
    def _():
        run(True)
        for hh in range(HG):
            o_ref[:, hh * hd:(hh + 1) * hd] = (acc_scr[hh] / l_scr[hh]).astype(o_ref.dtype)


def fox_prompt(q_hm, k_hm, v_hm, c_tok, c_t, *, B, T, H, hd):
    HG = _pick(H, 16)
    tq = tk = _pick(T, 512)
    nq = T // tq
    kv_spec = pl.BlockSpec((1, HG, tk, hd), lambda b, g, qi, ki: (b, g, jnp.minimum(ki, qi), 0))
    return pl.pallas_call(
        functools.partial(_fox_prompt_kernel, HG=HG, tq=tq, tk=tk, hd=hd),
        out_shape=jax.ShapeDtypeStruct((B * T, H * hd), BF16),
        grid=(B, H // HG, nq, nq),
        in_specs=[pl.BlockSpec((1, HG, tq, hd), lambda b, g, qi, ki: (b, g, qi, 0)),
                  kv_spec, kv_spec,
                  pl.BlockSpec((tq, LANES), lambda b, g, qi, ki: (b * nq + qi, 0)),
                  pl.BlockSpec((1, HG, tk), lambda b, g, qi, ki: (b, g, jnp.minimum(ki, qi)))],
        out_specs=pl.BlockSpec((tq, HG * hd), lambda b, g, qi, ki: (b * nq + qi, g)),
        scratch_shapes=[pltpu.VMEM((HG, tq, LANES), F32), pltpu.VMEM((HG, tq, LANES), F32),
                        pltpu.VMEM((HG, tq, hd), F32), pltpu.VMEM((HG, tq, LANES), F32)],
        compiler_params=_cparams(("parallel", "parallel", "parallel", "arbitrary")),
        name="fox_prompt",
    )(q_hm, k_hm, v_hm, c_tok, c_t)


def _fox_sample_kernel(q_ref, kn_ref, vn_ref, k_hbm, v_hbm, cq_ref, cn_ref, cp_ref, o_ref,
                       cq_scr, m_scr, l_scr, acc_scr, s_scr, p_scr, kbuf, vbuf, sem, *, H, hd, Ts, tk, layer):
    b = pl.program_id(0)
    j = pl.program_id(1)

    def tile_copies(tile, slot):
        span = pl.ds(tile * tk, tk)
        for hh in range(H):
            yield pltpu.make_async_copy(k_hbm.at[layer, b, span, hh, :], kbuf.at[slot, hh], sem.at[slot, 0])
            yield pltpu.make_async_copy(v_hbm.at[layer, b, span, hh, :], vbuf.at[slot, hh], sem.at[slot, 1])

    def rows(hh):
        return slice(hh * Ts, (hh + 1) * Ts)

    def cols(hh):
        return slice(hh * hd, (hh + 1) * hd)

    def update(width, key, value, ck, keep):
        for hh in range(H):
            s = lax.dot_general(q_ref[:, cols(hh)], key(hh), (((1,), (1,)), ((), ())),
                                preferred_element_type=F32)
            t = s - ck(hh)
            s_scr[rows(hh), :width] = t if keep is None else jnp.where(keep, t, -jnp.inf)
        m_new, l_new, alpha, p = _softmax_stats(s_scr[:, :width], cq_scr[...], m_scr[...], l_scr[...])
        m_scr[...] = m_new
        l_scr[...] = l_new
        p_scr[:, :width] = p.astype(BF16)
        for hh in range(H):
            acc_scr[rows(hh)] = alpha[rows(hh)] * acc_scr[rows(hh)] + jnp.dot(
                p_scr[rows(hh), :width], value(hh), preferred_element_type=F32)

    @pl.when(j == 0)
    def _():
        for cp in tile_copies(0, 0):
            cp.start()
        m_scr[...] = jnp.full_like(m_scr, -jnp.inf)
        l_scr[...] = jnp.zeros_like(l_scr)
        acc_scr[...] = jnp.zeros_like(acc_scr)
        for hh in range(H):
            cq_scr[rows(hh)] = jnp.broadcast_to(cq_ref[0, :, hh:hh + 1], (Ts, LANES))
        keep = (lax.broadcasted_iota(jnp.int32, (Ts, Ts), 0) >= lax.broadcasted_iota(jnp.int32, (Ts, Ts), 1))
        update(Ts, lambda hh: kn_ref[:, cols(hh)].astype(BF16), lambda hh: vn_ref[:, cols(hh)].astype(BF16),
               lambda hh: cn_ref[0, hh:hh + 1, :Ts], keep)

    @pl.when(j > 0)
    def _():
        tile = j - 1
        slot = tile % 2
        for cp in tile_copies(tile, slot):
            cp.wait()

        @pl.when(j < pl.num_programs(1) - 1)
        def _():
            for cp in tile_copies(tile + 1, 1 - slot):
                cp.start()

        update(tk, lambda hh: kbuf[slot, hh].astype(BF16), lambda hh: vbuf[slot, hh].astype(BF16),
               lambda hh: cp_ref[0, hh:hh + 1, :], None)

    @pl.when(j == pl.num_programs(1) - 1)
    def _():
        for hh in range(H):
            o_ref[:, cols(hh)] = (acc_scr[rows(hh)] / l_scr[rows(hh)]).astype(o_ref.dtype)


def fox_sample(q, k_new, v_new, k_past, v_past, c_tok, c_t, *, B, Ts, P, H, hd, layer):
    w = H * hd
    tk = _pick(P, 256)
    npk = P // tk
    assert P % LANES == 0 and Ts <= LANES and P % Ts == 0 and hd == LANES
    past = pl.BlockSpec(memory_space=pl.ANY)
    new = pl.BlockSpec((Ts, w), lambda b, j: (b, 0))
    return pl.pallas_call(
        functools.partial(_fox_sample_kernel, H=H, hd=hd, Ts=Ts, tk=tk, layer=layer),
        out_shape=jax.ShapeDtypeStruct((B * Ts, w), BF16),
        grid=(B, npk + 1),
        in_specs=[new, new, new, past, past,
                  pl.BlockSpec((1, Ts, LANES), lambda b, j: (b, P // Ts, 0)),
                  pl.BlockSpec((1, LANES, LANES), lambda b, j: (b, 0, P // LANES)),
                  pl.BlockSpec((1, LANES, tk), lambda b, j: (b, 0, jnp.maximum(j - 1, 0)))],
        out_specs=new,
        scratch_shapes=[pltpu.VMEM((H * Ts, LANES), F32), pltpu.VMEM((H * Ts, LANES), F32),
                        pltpu.VMEM((H * Ts, LANES), F32), pltpu.VMEM((H * Ts, hd), F32),
                        pltpu.VMEM((H * Ts, tk), F32), pltpu.VMEM((H * Ts, tk), BF16),
                        pltpu.VMEM((2, H, tk, hd), F32), pltpu.VMEM((2, H, tk, hd), F32),
                        pltpu.SemaphoreType.DMA((2, 2))],
        compiler_params=_cparams(("arbitrary", "arbitrary")),
        name="fox_sample",
    )(q, k_new, v_new, k_past, v_past, c_tok, c_t, c_t)


def fox_project(h, w_in, layer, wf, b_f, qn_g, kn_g, *, B, T, H, hd, head_major):
    m, _ = h.shape
    w = H * hd
    tm, tn = (_pick(T, 1024), _pick(w, 512)) if head_major else _mm_tiles(m, w)
    tpb = T // tm if head_major else 1
    tok = pl.BlockSpec((tm, tn), lambda i, j: (i, j))
    hm = pl.BlockSpec((1, tn // hd, tm, hd), lambda i, j: (i // tpb, j, i % tpb, 0))
    gain = pl.BlockSpec((1, hd), lambda i, j: (0, 0))
    tok_f32 = jax.ShapeDtypeStruct((m, w), F32)
    tok_bf16 = jax.ShapeDtypeStruct((m, w), BF16)
    hm_bf16 = jax.ShapeDtypeStruct((B, H, T, hd), BF16)

    def proj(section, g, norm, scale, shapes, specs, kinds, name):
        ep = functools.partial(_ep_heads, hd=hd, norm=norm, scale=scale, kinds=kinds)
        extras = (g.reshape(1, hd),) if norm else ()
        especs = (gain,) if norm else ()
        return matmul(h, w_in, layer=layer, col0=section * w, n=w, epilogue=ep, extras=extras,
                      extra_specs=especs, out_shapes=shapes, out_specs=specs, tm=tm, tn=tn, name=name)

    q_scale = hd ** -0.5 * LOG2E
    if head_major:
        (q,) = proj(0, qn_g, True, q_scale, (hm_bf16,), (hm,), ("head",), "fox_q")
        k, k_hm = proj(1, kn_g, True, 1.0, (tok_f32, hm_bf16), (tok, hm), ("tok", "head"), "fox_k")
        v, v_hm = proj(2, None, False, 1.0, (tok_f32, hm_bf16), (tok, hm), ("tok", "head"), "fox_v")
    else:
        (q,) = proj(0, qn_g, True, q_scale, (tok_bf16,), (tok,), ("tok",), "fox_q")
        (k,) = proj(1, kn_g, True, 1.0, (tok_f32,), (tok,), ("tok",), "fox_k")
        (v,) = proj(2, None, False, 1.0, (tok_f32,), (tok,), ("tok",), "fox_v")
        k_hm = v_hm = None
    tmf = _pick(m, 1024)
    logf = matmul(h, wf, layer=0, n=LANES, epilogue=_ep_logsig, extras=(b_f,),
                  extra_specs=(pl.BlockSpec((1, LANES), lambda i, j: (0, 0)),),
                  out_shapes=jax.ShapeDtypeStruct((m, LANES), F32),
                  out_specs=pl.BlockSpec((tmf, LANES), lambda i, j: (i, 0)),
                  tm=tmf, tn=LANES, name="fox_logf")
    return q, k, v, logf, k_hm, v_hm


def _stack(parts):
    return parts[0][None] if len(parts) == 1 else jnp.stack(parts)


def _rope_tables(pos, hd):
    half = hd // 2
    inv = ROPE_BASE ** (-jnp.arange(half, dtype=F32) / half)
    ang = pos.astype(F32)[:, None] * inv[None, :]
    return jnp.cos(ang), jnp.sin(ang)


def kernel(x_prompt, x_sample, mem_prompt, state_ret, cache_fox_k, cache_fox_v, cache_fox_logf, cache_mem_k, cache_mem_v, norm_mix, norm_mem, norm_moe, even_w_in, ret_gn_g, cmlp_ln_g, cmlp_ln_b, cmlp_ws, cmlp_bs, even_w_out, odd_w_in, fox_b_f, fox_qn_g, fox_kn_g, odd_w_out, mem_src_g, mem_w_q, mem_w_k, mem_w_v, mem_qn_g, mem_kn_g, mem_w_o, moe_w_rg, moe_b_rg, moe_w_re, moe_b_re, moe_w_gate, moe_w_up, moe_w_down):
    Bp, Tp, D = x_prompt.shape
    Bs, Ts, _ = x_sample.shape
    depth = norm_mix.shape[0]
    P = cache_fox_k.shape[2]
    RH, RHD = state_ret.shape[2], state_ret.shape[3]
    RW = RH * RHD
    CW = cmlp_ln_g.shape[1]
    FH, FHD = cache_fox_k.shape[3], cache_fox_k.shape[4]
    FW = FH * FHD
    MH, MHD = cache_mem_k.shape[3], cache_mem_k.shape[4]
    MLEN = mem_prompt.shape[1]
    NG, NE, _, DF = moe_w_gate.shape[1:]
    assert (4 * RW) % CW == 0 and FHD == LANES and FH <= LANES

    xp = x_prompt.reshape(Bp * Tp, D)
    xs = x_sample.reshape(Bs * Ts, D)
    mem2d = mem_prompt.reshape(Bp * MLEN, D)
    cos_p, sin_p = _rope_tables(jnp.arange(Tp), RHD)
    cos_s, sin_s = _rope_tables(P + jnp.arange(Ts), RHD)

    wg_all = moe_w_gate.reshape(depth * NG * NE, D, DF).astype(BF16)
    wu_all = moe_w_up.reshape(depth * NG * NE, D, DF).astype(BF16)
    wd_all = moe_w_down.reshape(depth * NG * NE, DF, D).astype(BF16)

    ret_S_p, ret_S_s, cmlp_v_s = [], [], []
    fk_p, fv_p, fl_p, fk_s, fv_s, fl_s = [], [], [], [], [], []
    mk_p, mv_p = [], []
    for i in range(depth):
        j = i // 2
        if i == 0:
            hp = rmsnorm_rows(xp, norm_mix[i])
            hs = rmsnorm_rows(xs, norm_mix[i])
        if i % 2 == 0:
            def even(h, x, cos, sin, s0, B, T, want_zv, head_major):
                if head_major:
                    tm, tn = _pick(T, 1024), _pick(4 * RW, 512)
                    tpb = T // tm
                    (proj_ret,) = matmul(
                        h, even_w_in, layer=j, n=4 * RW,
                        epilogue=functools.partial(_ep_heads, hd=RHD, norm=False, scale=1.0, kinds=("head",)),
                        out_shapes=(jax.ShapeDtypeStruct((B, 4 * RH, T, RHD), F32),),
                        out_specs=(pl.BlockSpec((1, tn // RHD, tm, RHD),
                                                lambda a, b: (a // tpb, b, a % tpb, 0)),),
                        tm=tm, tn=tn, name="even_in_ret")
                else:
                    proj_ret = matmul_plain(h, even_w_in, F32, "even_in_ret", layer=j, n=4 * RW)
                proj_mlp = matmul_plain(h, even_w_in, F32, "even_in_mlp", layer=j, col0=4 * RW)
                y_a, S = retention(proj_ret, cos, sin, ret_gn_g[j], s0, B=B, T=T, H=RH, hd=RHD)
                y_b, zv = cmlp(proj_mlp, cmlp_ln_g[j], cmlp_ln_b[j], cmlp_ws[j], cmlp_bs[j],
                               B=B, T=T, u_col=0, want_zv=want_zv)
                return matmul2_residual(y_a, y_b, even_w_out, x, "even_out", layer=j), S, zv

            xp, Sp, _ = even(hp, xp, cos_p, sin_p, None, Bp, Tp, False, True)
            xs, Ss, zvs = even(hs, xs, cos_s, sin_s, state_ret[j], Bs, Ts, True, False)
            ret_S_p.append(Sp)
            ret_S_s.append(Ss)
            cmlp_v_s.append(zvs.reshape(Bs, Ts, CW))
        else:
            wf = jnp.pad(odd_w_in[j, :, 3 * FW:], ((0, 0), (0, LANES - FH)))[None]
            b_f = jnp.pad(fox_b_f[j], (0, LANES - FH)).reshape(1, LANES)
            q, k, v, logf, k_hm, v_hm = fox_project(hp, odd_w_in, j, wf, b_f, fox_qn_g[j], fox_kn_g[j],
                                                    B=Bp, T=Tp, H=FH, hd=FHD, head_major=True)
            c_tok, c_t = cumsum_time(logf.reshape(Bp, Tp, LANES))
            o = fox_prompt(q, k_hm, v_hm, c_tok.reshape(Bp * Tp, LANES), c_t[:, :FH], B=Bp, T=Tp, H=FH, hd=FHD)
            xp = matmul_plain(o, odd_w_out, F32, "odd_out", layer=j, residual=xp)
            fk_p.append(k.reshape(Bp, Tp, FH, FHD))
            fv_p.append(v.reshape(Bp, Tp, FH, FHD))
            fl_p.append(logf[:, :FH].reshape(Bp, Tp, FH))
            q, k, v, logf, _, _ = fox_project(hs, odd_w_in, j, wf, b_f, fox_qn_g[j], fox_kn_g[j],
                                              B=Bs, T=Ts, H=FH, hd=FHD, head_major=False)
            t_pad = -(-(P + Ts) // 256) * 256
            seq = jnp.concatenate([jnp.pad(cache_fox_logf[j], ((0, 0), (0, 0), (0, LANES - FH))),
                                   logf.reshape(Bs, Ts, LANES),
                                   jnp.zeros((Bs, t_pad - P - Ts, LANES), F32)], axis=1)
            c_tok, c_t = cumsum_time(seq)
            o = fox_sample(q, k, v, cache_fox_k, cache_fox_v, c_tok, c_t,
                           B=Bs, Ts=Ts, P=P, H=FH, hd=FHD, layer=j)
            xs = matmul_plain(o, odd_w_out, F32, "odd_out", layer=j, residual=xs)
            fk_s.append(k.reshape(Bs, Ts, FH, FHD))
            fv_s.append(v.reshape(Bs, Ts, FH, FHD))
            fl_s.append(logf[:, :FH].reshape(Bs, Ts, FH))
        m_n = rmsnorm_rows(mem2d, mem_src_g[i])
        mw = MH * MHD
        tmm, tnm = _pick(Bp * MLEN, 1024), _pick(mw, 512)
        tile = pl.BlockSpec((tmm, tnm), lambda a, b: (a, b))
        (mk,) = matmul(m_n, mem_w_k, layer=i, n=mw,
                       epilogue=functools.partial(_ep_heads, hd=MHD, norm=True, scale=1.0, kinds=("tok",)),
                       extras=(mem_kn_g[i].reshape(1, MHD),),
                       extra_specs=(pl.BlockSpec((1, MHD), lambda a, b: (0, 0)),),
                       out_shapes=(jax.ShapeDtypeStruct((Bp * MLEN, mw), F32),), out_specs=(tile,),
                       tm=tmm, tn=tnm, name="mem_k")
        mv = matmul_plain(m_n, mem_w_v, F32, "mem_v", layer=i)
        mk_p.append(mk.reshape(Bp, MLEN, MH, MHD))
        mv_p.append(mv.reshape(Bp, MLEN, MH, MHD))
        wq_m = mem_w_q[i].astype(BF16)
        wo_m = mem_w_o[i].astype(BF16)
        xp = mem_attn(xp, norm_mem[i], wq_m, mem_qn_g[i], mk.reshape(Bp, MLEN, mw), mv.reshape(Bp, MLEN, mw),
                      wo_m, B=Bp, T=Tp, heads=MH)
        xs = mem_attn(xs, norm_mem[i], wq_m, mem_qn_g[i], cache_mem_k[i].reshape(Bs, MLEN, mw),
                      cache_mem_v[i].reshape(Bs, MLEN, mw), wo_m, B=Bs, T=Ts, heads=MH)
        moe = (norm_moe[i], moe_w_rg[i], moe_b_rg[i], moe_w_re[i], moe_b_re[i], wg_all, wu_all, wd_all)
        next_g = norm_mix[i + 1] if i + 1 < depth else None
        xp, hp = hier_moe(xp, *moe, first_expert=i * NG * NE, n_experts=NG * NE, next_g=next_g)
        xs, hs = hier_moe(xs, *moe, first_expert=i * NG * NE, n_experts=NG * NE, next_g=next_g)

    return (xp.reshape(Bp, Tp, D), xs.reshape(Bs, Ts, D),
            _stack(ret_S_p), _stack(ret_S_s), _stack(cmlp_v_s),
            _stack(fk_p), _stack(fv_p), _stack(fl_p),
            _stack(fk_s), _stack(fv_s), _stack(fl_s),
            _stack(mk_p), _stack(mv_p))
```
